```python
import jax, jax.numpy as jnp
from jax import lax
import numpy as np

D_MODEL = 1024
BATCH = 8
SEQ = 2048
DEPTH = 1

GRID_W = 64
CTX_LEN = 256
N_MOD = 6
HG_HEADS = 4
HG_DK = 128
HG_DV = 128
GLA_HEADS = 4
GLA_DK = 128
GLA_DV = 128
GLA_RANK = 16
GLA_GATE_NORM = 16.0
HG_W = HG_HEADS * HG_DK
HG_VW = HG_HEADS * HG_DV
GLA_KW = GLA_HEADS * GLA_DK
GLA_VW = GLA_HEADS * GLA_DV
D_FF = ((8 * D_MODEL // 3 + 255) // 256) * 256
EPS = 1e-6
IN_SPLITS = (HG_W, HG_VW, HG_W, HG_W, HG_VW, GLA_KW, GLA_KW, GLA_VW, GLA_VW, GLA_RANK, GLA_RANK, D_MODEL, D_MODEL)
IN_WIDTH = sum(IN_SPLITS)
IN_OFFSETS = tuple(int(v) for v in np.cumsum(IN_SPLITS)[:-1])

kernel_name = "hybrid_hgrn2_gla_prefix_dit_block"


def rms_norm(a, w):
    af = a.astype(jnp.float32)
    return (af * lax.rsqrt(jnp.mean(af * af, axis=-1, keepdims=True) + EPS)).astype(a.dtype) * w


def modulate(h, shift_c, scale_c, shift_x, scale_x):
    hc, hx = h[:, :CTX_LEN], h[:, CTX_LEN:]
    return jnp.concatenate([hc * (1 + scale_c) + shift_c,
                            hx * (1 + scale_x[:, None]) + shift_x[:, None]], axis=1)


def apply_gate(h, gate_c, gate_x):
    return jnp.concatenate([h[:, :CTX_LEN] * gate_c, h[:, CTX_LEN:] * gate_x[:, None]], axis=1)


def to_heads(a, n_heads):
    b, t, w = a.shape
    return a.reshape(b, t, n_heads, w // n_heads).transpose(0, 2, 1, 3)


def merge_heads(a):
    b, h, t, d = a.shape
    return a.transpose(0, 2, 1, 3).reshape(b, t, h * d)


def segment_reverse(a):
    return jnp.concatenate([jnp.flip(a[:, :, :CTX_LEN], axis=2), jnp.flip(a[:, :, CTX_LEN:], axis=2)], axis=2)


def chunk_scan(q, k, v, log_f, n_chunks):
    b, h, t, dk = q.shape
    dv = v.shape[-1]

    def chunks(a):
        return jnp.moveaxis(a.reshape(b, h, n_chunks, GRID_W, a.shape[-1]), 2, 0)

    causal = jnp.tril(jnp.ones((GRID_W, GRID_W), dtype=bool))[:, :, None]

    def step(s, blk):
        qc, kc, vc, gc = blk
        cum = jnp.cumsum(gc.astype(jnp.float32), axis=2)
        pair = jnp.exp(jnp.where(causal, cum[:, :, :, None, :] - cum[:, :, None, :, :], -jnp.inf))
        scores = jnp.einsum('bhtk,bhtsk,bhsk->bhts', qc, pair, kc)
        o = (jnp.einsum('bhts,bhsv->bhtv', scores, vc)
             + jnp.einsum('bhtk,bhkv->bhtv', qc * jnp.exp(cum), s))
        tail = jnp.exp(cum[:, :, -1:, :] - cum)
        s_new = (s * jnp.exp(cum[:, :, -1, :])[..., None]
                 + jnp.einsum('bhsk,bhsv->bhkv', kc * tail, vc))
        return s_new, o

    s0 = jnp.zeros((b, h, dk, dv), jnp.float32)
    _, o = lax.scan(step, s0, (chunks(q), chunks(k), chunks(v), chunks(log_f)))
    return jnp.moveaxis(o, 0, 2).reshape(b, h, t, dv).astype(v.dtype)


def bidirectional_scan(q, k_fw, k_bw, v, lf_fw, lf_bw, n_chunks):
    o_fw = chunk_scan(q, k_fw, v, lf_fw, n_chunks)
    o_bw = chunk_scan(segment_reverse(q), segment_reverse(k_bw), segment_reverse(v),
                      segment_reverse(lf_bw), n_chunks)
    return o_fw + segment_reverse(o_bw)


def hybrid_mixer(h, w_in, lb, hg_onorm, gla_w_gk, gla_b_gk, gla_onorm, w_br_hg, w_br_gla, w_out, n_chunks):
    (hq, hi, hf_fw, hf_bw, hg_gate, gq, gk, gv, g_gate,
     lr_fw, lr_bw, gate_hg, gate_gla) = jnp.split(h @ w_in, IN_OFFSETS, axis=-1)

    def hg_forget(raw, lb_dir):
        f = lb_dir + (1 - lb_dir) * jax.nn.sigmoid(raw.astype(jnp.float32))
        return to_heads(1 - f, HG_HEADS), to_heads(jnp.log(f), HG_HEADS)

    q = to_heads(jax.nn.silu(hq), HG_HEADS)
    i = to_heads(hi, HG_HEADS)
    k_fw, lf_fw = hg_forget(hf_fw, lb[0])
    k_bw, lf_bw = hg_forget(hf_bw, lb[1])
    o = bidirectional_scan(q, k_fw, k_bw, i, lf_fw, lf_bw, n_chunks)
    o_hg = merge_heads(rms_norm(o, hg_onorm)) * jax.nn.silu(hg_gate)

    def gla_gate_log(lr, w, bias):
        return to_heads(jax.nn.log_sigmoid((lr @ w + bias).astype(jnp.float32)) / GLA_GATE_NORM, GLA_HEADS)

    q = to_heads(gq, GLA_HEADS) * GLA_DK ** -0.5
    k = to_heads(gk, GLA_HEADS)
    v = to_heads(gv, GLA_HEADS)
    lf_fw = gla_gate_log(lr_fw, gla_w_gk[0], gla_b_gk[0])
    lf_bw = gla_gate_log(lr_bw, gla_w_gk[1], gla_b_gk[1])
    o = bidirectional_scan(q, k, k, v, lf_fw, lf_bw, n_chunks)
    o_gla = merge_heads(rms_norm(o, gla_onorm)) * jax.nn.silu(g_gate)

    merged = (jax.nn.sigmoid(gate_hg) * (o_hg @ w_br_hg)
              + jax.nn.sigmoid(gate_gla) * (o_gla @ w_br_gla))
    return merged @ w_out


def swiglu(h, w_gate, w_up, w_down):
    return (jax.nn.silu(h @ w_gate) * (h @ w_up)) @ w_down


def setup_inputs(seed: int = 0) -> dict:
    key = jax.random.key(seed)
    ks = jax.random.split(key, 24)

    def nrm(k, shape, scale):
        return jax.random.normal(k, shape, jnp.float32) * scale

    def gain(k, shape):
        return 1.0 + nrm(k, shape, 0.05)

    return {
        "x": nrm(ks[0], (BATCH, SEQ, D_MODEL), 1.0),
        "c": nrm(ks[1], (BATCH, D_MODEL), 1.0),
        "ctx": nrm(ks[2], (BATCH, CTX_LEN, D_MODEL), 1.0),
        "c_ctx": nrm(ks[3], (D_MODEL,), 1.0),
        "w_mod": nrm(ks[4], (DEPTH, D_MODEL, N_MOD * D_MODEL), 0.5 * D_MODEL ** -0.5),
        "b_mod": nrm(ks[5], (DEPTH, N_MOD * D_MODEL), 0.01),
        "norm_pre1": gain(ks[6], (DEPTH, D_MODEL)),
        "norm_post1": gain(ks[7], (DEPTH, D_MODEL)),
        "norm_pre2": gain(ks[8], (DEPTH, D_MODEL)),
        "norm_post2": gain(ks[9], (DEPTH, D_MODEL)),
        "w_in": nrm(ks[10], (DEPTH, D_MODEL, IN_WIDTH), D_MODEL ** -0.5),
        "hg_lb": nrm(ks[11], (DEPTH + 1, 2, HG_W), 1.0),
        "hg_onorm": gain(ks[12], (DEPTH, HG_DV)),
        "gla_w_gk": nrm(ks[13], (DEPTH, 2, GLA_RANK, GLA_KW), GLA_RANK ** -0.5),
        "gla_b_gk": nrm(ks[14], (DEPTH, 2, GLA_KW), 0.1),
        "gla_onorm": gain(ks[15], (DEPTH, GLA_DV)),
        "w_br_hg": nrm(ks[16], (DEPTH, HG_VW, D_MODEL), HG_VW ** -0.5),
        "w_br_gla": nrm(ks[17], (DEPTH, GLA_VW, D_MODEL), GLA_VW ** -0.5),
        "w_out": nrm(ks[18], (DEPTH, D_MODEL, D_MODEL), D_MODEL ** -0.5),
        "w_ff_gate": nrm(ks[19], (DEPTH, D_MODEL, D_FF), D_MODEL ** -0.5),
        "w_ff_up": nrm(ks[20], (DEPTH, D_MODEL, D_FF), D_MODEL ** -0.5),
        "w_ff_down": nrm(ks[21], (DEPTH, D_FF, D_MODEL), D_FF ** -0.5),
    }


def reference(x, c, ctx, c_ctx, w_mod, b_mod, norm_pre1, norm_post1, norm_pre2, norm_post2, w_in, hg_lb,
              hg_onorm, gla_w_gk, gla_b_gk, gla_onorm, w_br_hg, w_br_gla, w_out, w_ff_gate, w_ff_up, w_ff_down):
    rows = x.shape[1] // GRID_W
    n_chunks = CTX_LEN // GRID_W + rows
    z = jnp.concatenate([ctx, x], axis=1)
    lb_all = jnp.cumsum(jax.nn.softmax(hg_lb.astype(jnp.float32), axis=0), axis=0)
    for l in range(DEPTH):
        m_c = jnp.split(jax.nn.silu(c_ctx) @ w_mod[l] + b_mod[l], N_MOD, axis=-1)
        m_x = jnp.split(jax.nn.silu(c) @ w_mod[l] + b_mod[l], N_MOD, axis=-1)
        h = modulate(rms_norm(z, norm_pre1[l]), m_c[0], m_c[1], m_x[0], m_x[1])
        y = hybrid_mixer(h, w_in[l], lb_all[l], hg_onorm[l], gla_w_gk[l], gla_b_gk[l], gla_onorm[l],
                         w_br_hg[l], w_br_gla[l], w_out[l], n_chunks)
        z = z + apply_gate(rms_norm(y, norm_post1[l]), m_c[2], m_x[2])
        h = modulate(rms_norm(z, norm_pre2[l]), m_c[3], m_c[4], m_x[3], m_x[4])
        y = swiglu(h, w_ff_gate[l], w_ff_up[l], w_ff_down[l])
        z = z + apply_gate(rms_norm(y, norm_post2[l]), m_c[5], m_x[5])
    return z[:, CTX_LEN:]
```

```python
import functools

import jax
import jax.numpy as jnp
from jax import lax
from jax.experimental import pallas as pl
from jax.experimental.pallas import tpu as pltpu

F32 = jnp.float32
BF16 = jnp.bfloat16

EPS = 1e-6
CHUNK = 64
N_MOD = 6
HEADS = 4
HEAD_DIM = 128
BRANCH_W = HEADS * HEAD_DIM
GLA_RANK = 16
GLA_GATE_NORM = 16.0
LR_PAD = 128
MOD_ROWS = 16
VMEM_LIMIT_BYTES = 56 * 1024 * 1024

SB_HQ, SB_HI, SB_HKF, SB_HKB, SB_GQ, SB_GK, SB_GV = (i * BRANCH_W for i in range(7))
SB_W = 7 * BRANCH_W
SF_HF, SF_HB, SF_GF, SF_GB = (i * BRANCH_W for i in range(4))
SF_W = 4 * BRANCH_W


def _dot(a, b):
    return jnp.dot(a, b, preferred_element_type=F32)


def _dot_nt(a, b):
    return lax.dot_general(a, b, (((1,), (1,)), ((), ())), preferred_element_type=F32)


def _dot_tn(a, b):
    return lax.dot_general(a, b, (((0,), (0,)), ((), ())), preferred_element_type=F32)


def _sigmoid(x):
    return 1.0 / (1.0 + jnp.exp(-x))


def _silu(x):
    return x * _sigmoid(x)


def _rms(x):
    return x * lax.rsqrt(jnp.mean(x * x, axis=-1, keepdims=True) + EPS)


def _const_spec(shape):
    nd = len(shape)
    return pl.BlockSpec(shape, lambda *_: (0,) * nd, pipeline_mode=pl.Buffered(1))


def _mod_kernel(c_ref, w_ref, b_ref, o_ref):
    a = _silu(c_ref[...]).astype(BF16)
    o_ref[...] = _dot(a, w_ref[...].astype(BF16)) + b_ref[...]


def _modulation(c_all, w_mod, b_mod):
    d = c_all.shape[1]
    n = w_mod.shape[1]
    blk = 1024
    return pl.pallas_call(
        _mod_kernel,
        grid=(n // blk,),
        in_specs=[pl.BlockSpec((MOD_ROWS, d), lambda j: (0, 0)),
                  pl.BlockSpec((d, blk), lambda j: (0, j)),
                  pl.BlockSpec((1, blk), lambda j: (0, j))],
        out_specs=pl.BlockSpec((MOD_ROWS, blk), lambda j: (0, j)),
        out_shape=jax.ShapeDtypeStruct((MOD_ROWS, n), F32),
        name="modulation",
    )(c_all, w_mod, b_mod)


def _inproj_kernel(x_ref, ctx_ref, mod_ref, pre_ref, w_ref, lb_ref, wgk_ref, bgk_ref,
                   sb_ref, sf_ref, gb_ref, *, d_model):
    j = pl.program_id(1)
    z = jnp.where(j == 0, ctx_ref[0], x_ref[0])
    mod = mod_ref[0]
    shift, scale = mod[:, :d_model], mod[:, d_model:2 * d_model]
    h = (_rms(z) * pre_ref[...] * (1.0 + scale) + shift).astype(BF16)

    lbr = lb_ref[...]
    e = jnp.exp(lbr - jnp.max(lbr, axis=0, keepdims=True))
    lb = e[0] / jnp.sum(e, axis=0)

    def proj(g):
        return _dot(h, w_ref[:, g * BRANCH_W:(g + 1) * BRANCH_W])

    sb_ref[0, :, SB_HQ:SB_HQ + BRANCH_W] = _silu(proj(0)).astype(BF16)
    sb_ref[0, :, SB_HI:SB_HI + BRANCH_W] = proj(1).astype(BF16)
    for d, (g, kcol, fcol) in enumerate(((2, SB_HKF, SF_HF), (3, SB_HKB, SF_HB))):
        raw = proj(g)
        lbd = lb[d:d + 1]
        en = jnp.exp(-raw)
        sig = 1.0 / (1.0 + en)
        f = lbd + (1.0 - lbd) * sig
        sb_ref[0, :, kcol:kcol + BRANCH_W] = (1.0 - f).astype(BF16)
        sf_ref[0, :, fcol:fcol + BRANCH_W] = jnp.log(f)
    gb_ref[0, :, 0:BRANCH_W] = _silu(proj(4)).astype(BF16)
    sb_ref[0, :, SB_GQ:SB_GQ + BRANCH_W] = (proj(5) * HEAD_DIM ** -0.5).astype(BF16)
    sb_ref[0, :, SB_GK:SB_GK + BRANCH_W] = proj(6).astype(BF16)
    sb_ref[0, :, SB_GV:SB_GV + BRANCH_W] = proj(7).astype(BF16)
    gb_ref[0, :, BRANCH_W:2 * BRANCH_W] = _silu(proj(8)).astype(BF16)
    for g in range(9, 13):
        c0 = 2 * BRANCH_W + (g - 9) * BRANCH_W
        gb_ref[0, :, c0:c0 + BRANCH_W] = _sigmoid(proj(g)).astype(BF16)

    lr = _dot(h, w_ref[:, 13 * BRANCH_W:13 * BRANCH_W + LR_PAD]).astype(BF16)
    xg = _dot(lr, wgk_ref[...]) + bgk_ref[...]
    ls = jnp.minimum(xg, 0.0) - jnp.log(1.0 + jnp.exp(-jnp.abs(xg)))
    sf_ref[0, :, SF_GF:SF_GF + 2 * BRANCH_W] = ls * (1.0 / GLA_GATE_NORM)


def _in_projection(x, ctx, mod3, norm_pre, w_in_r, hg_lb, wgk, bgk, *, tm):
    b, seq, d = x.shape
    ctx_len = ctx.shape[1]
    assert ctx_len == tm
    t_all = ctx_len + seq
    n_lat = seq // tm
    nw = w_in_r.shape[1]
    kern = functools.partial(_inproj_kernel, d_model=d)
    return pl.pallas_call(
        kern,
        grid=(b, n_lat + 1),
        in_specs=[
            pl.BlockSpec((1, tm, d), lambda i, j: (i, jnp.maximum(j - 1, 0), 0)),
            pl.BlockSpec((1, tm, d), lambda i, j: (i, 0, 0)),
            pl.BlockSpec((1, 1, N_MOD * d), lambda i, j: (jnp.where(j == 0, 0, i + 1), 0, 0)),
            _const_spec((1, d)),
            _const_spec((d, nw)),
            _const_spec(hg_lb.shape),
            _const_spec(wgk.shape),
            _const_spec(bgk.shape),
        ],
        out_specs=[
            pl.BlockSpec((1, tm, SB_W), lambda i, j: (i, j, 0)),
            pl.BlockSpec((1, tm, SF_W), lambda i, j: (i, j, 0)),
            pl.BlockSpec((1, tm, 6 * BRANCH_W), lambda i, j: (i, j, 0)),
        ],
        out_shape=[
            jax.ShapeDtypeStruct((b, t_all, SB_W), BF16),
            jax.ShapeDtypeStruct((b, t_all, SF_W), F32),
            jax.ShapeDtypeStruct((b, t_all, 6 * BRANCH_W), BF16),
        ],
        compiler_params=pltpu.CompilerParams(
            dimension_semantics=("arbitrary", "arbitrary"),
            vmem_limit_bytes=VMEM_LIMIT_BYTES),
        name="in_projection",
    )(x, ctx, mod3, norm_pre, w_in_r, hg_lb, wgk, bgk)


def _scan_branch(q, k, v, lf, tri, mask, mid_row, last_row, s_ref, base, with_out):
    hi = lf.astype(BF16)
    r1 = lf - hi.astype(F32)
    md = r1.astype(BF16)
    lo = (r1 - md.astype(F32)).astype(BF16)
    c = _dot(tri, hi) + _dot(tri, md) + _dot(tri, lo)
    r = c[mid_row:mid_row + 1]
    cl = c[last_row:last_row + 1]
    kt = k.astype(F32) * jnp.exp(r - c)
    kd = (kt * jnp.exp(cl - r)).astype(BF16)
    e_last = jnp.exp(cl)
    if with_out:
        qt = q.astype(F32) * jnp.exp(c - r)
        qd = (qt * jnp.exp(r)).astype(BF16)
        qt = qt.astype(BF16)
        kt = kt.astype(BF16)
    outs = []
    for h in range(HEADS):
        sl = slice(h * HEAD_DIM, (h + 1) * HEAD_DIM)
        st = s_ref[base + h]
        if with_out:
            a = _dot_nt(qt[:, sl], kt[:, sl])
            a = jnp.where(mask, a, 0.0).astype(BF16)
            outs.append(_dot(a, v[:, sl]) + _dot_nt(qd[:, sl], st.astype(BF16)))
        s_ref[base + h] = st * e_last[:, sl] + _dot_tn(v[:, sl], kd[:, sl])
    if with_out:
        return jnp.concatenate(outs, axis=-1)
    return None


def _scan_kernel(sbf_ref, sff_ref, sbb_ref, sfb_ref, o_ref, s_ref, *, ctx_chunks, n_chunks):
    s = pl.program_id(1)

    @pl.when(s == 0)
    def _():
        s_ref[...] = jnp.zeros_like(s_ref)
        o_ref[...] = jnp.zeros_like(o_ref)

    row = lax.broadcasted_iota(jnp.int32, (CHUNK, CHUNK), 0)
    col = lax.broadcasted_iota(jnp.int32, (CHUNK, CHUNK), 1)
    mask_f = col <= row
    mask_b = col >= row
    tri_f = jnp.where(mask_f, 1.0, 0.0).astype(BF16)
    tri_b = jnp.where(mask_b, 1.0, 0.0).astype(BF16)
    half = CHUNK // 2

    def run(with_out):
        res = []
        for d, (sb_ref, sf_ref, tri, mask, mid_row, last_row, kcol, hf, gf) in enumerate((
                (sbf_ref, sff_ref, tri_f, mask_f, half, CHUNK - 1, SB_HKF, SF_HF, SF_GF),
                (sbb_ref, sfb_ref, tri_b, mask_b, half - 1, 0, SB_HKB, SF_HB, SF_GB))):
            o_h = _scan_branch(sb_ref[0, :, SB_HQ:SB_HQ + BRANCH_W],
                               sb_ref[0, :, kcol:kcol + BRANCH_W],
                               sb_ref[0, :, SB_HI:SB_HI + BRANCH_W],
                               sf_ref[0, :, hf:hf + BRANCH_W],
                               tri, mask, mid_row, last_row, s_ref, d * HEADS, with_out)
            o_g = _scan_branch(sb_ref[0, :, SB_GQ:SB_GQ + BRANCH_W],
                               sb_ref[0, :, SB_GK:SB_GK + BRANCH_W],
                               sb_ref[0, :, SB_GV:SB_GV + BRANCH_W],
                               sf_ref[0, :, gf:gf + BRANCH_W],
                               tri, mask, mid_row, last_row, s_ref, (2 + d) * HEADS, with_out)
            res.append((o_h, o_g))
        return res

    @pl.when(s < ctx_chunks)
    def _():
        run(False)

    @pl.when(s >= ctx_chunks)
    def _():
        (of_h, of_g), (ob_h, ob_g) = run(True)
        rf = pl.multiple_of((s - ctx_chunks) * CHUNK, CHUNK)
        rb = pl.multiple_of((n_chunks - 1 - s) * CHUNK, CHUNK)
        o_ref[0, pl.ds(rf, CHUNK), :] += jnp.concatenate([of_h, of_g], axis=-1)
        o_ref[0, pl.ds(rb, CHUNK), :] += jnp.concatenate([ob_h, ob_g], axis=-1)


def _bidirectional_scan(sb, sf, *, ctx_len, seq):
    b = sb.shape[0]
    ctx_chunks = ctx_len // CHUNK
    n_chunks = (ctx_len + seq) // CHUNK

    def fw(i, s):
        return (i, s, 0)

    def bw(i, s):
        return (i, jnp.where(s < ctx_chunks, ctx_chunks - 1 - s, n_chunks + ctx_chunks - 1 - s), 0)

    kern = functools.partial(_scan_kernel, ctx_chunks=ctx_chunks, n_chunks=n_chunks)
    return pl.pallas_call(
        kern,
        grid=(b, n_chunks),
        in_specs=[pl.BlockSpec((1, CHUNK, SB_W), fw), pl.BlockSpec((1, CHUNK, SF_W), fw),
                  pl.BlockSpec((1, CHUNK, SB_W), bw), pl.BlockSpec((1, CHUNK, SF_W), bw)],
        out_specs=pl.BlockSpec((1, seq, 2 * BRANCH_W), lambda i, s: (i, 0, 0)),
        out_shape=jax.ShapeDtypeStruct((b, seq, 2 * BRANCH_W), F32),
        scratch_shapes=[pltpu.VMEM((4 * HEADS, HEAD_DIM, HEAD_DIM), F32)],
        compiler_params=pltpu.CompilerParams(
            dimension_semantics=("arbitrary", "arbitrary"),
            vmem_limit_bytes=VMEM_LIMIT_BYTES),
        name="bidirectional_scan",
    )(sb, sf, sb, sf)


def _tail_kernel(o_ref, gb_ref, x_ref, mod_ref, hgn_ref, gln_ref, post1_ref, pre2_ref, post2_ref,
                 wbh_ref, wbg_ref, wout_ref, wg_ref, wu_ref, wd_ref, out_ref, *, d_model):
    o = o_ref[0]
    parts = []
    for h in range(2 * HEADS):
        oh = o[:, h * HEAD_DIM:(h + 1) * HEAD_DIM]
        w = hgn_ref[...] if h < HEADS else gln_ref[...]
        parts.append(_rms(oh) * w)
    on = jnp.concatenate(parts, axis=-1)
    og = (on * gb_ref[0, :, 0:2 * BRANCH_W].astype(F32)).astype(BF16)
    yh = _dot(og[:, :BRANCH_W], wbh_ref[...])
    yg = _dot(og[:, BRANCH_W:], wbg_ref[...])
    c0 = 2 * BRANCH_W
    merged = (gb_ref[0, :, c0:c0 + d_model].astype(F32) * yh
              + gb_ref[0, :, c0 + d_model:c0 + 2 * d_model].astype(F32) * yg)
    y = _dot(merged.astype(BF16), wout_ref[...])

    mod = mod_ref[0]
    m = [mod[:, i * d_model:(i + 1) * d_model] for i in range(N_MOD)]
    z1 = x_ref[0] + _rms(y) * post1_ref[...] * m[2]
    h2 = (_rms(z1) * pre2_ref[...] * (1.0 + m[4]) + m[3]).astype(BF16)
    g = _dot(h2, wg_ref[...])
    u = _dot(h2, wu_ref[...])
    a = (_silu(g) * u).astype(BF16)
    y2 = _dot(a, wd_ref[...])
    out_ref[0] = z1 + _rms(y2) * post2_ref[...] * m[5]


def _tail(o, gb, x, mod3, hgn, gln, post1, pre2, post2, wbh, wbg, wout, wg, wu, wd, *, tm, ctx_len):
    b, seq, d = x.shape
    off = ctx_len // tm
    kern = functools.partial(_tail_kernel, d_model=d)
    consts = (hgn, gln, post1, pre2, post2, wbh, wbg, wout, wg, wu, wd)
    return pl.pallas_call(
        kern,
        grid=(b, seq // tm),
        in_specs=[
            pl.BlockSpec((1, tm, 2 * BRANCH_W), lambda i, j: (i, j, 0)),
            pl.BlockSpec((1, tm, 6 * BRANCH_W), lambda i, j: (i, j + off, 0)),
            pl.BlockSpec((1, tm, d), lambda i, j: (i, j, 0)),
            pl.BlockSpec((1, 1, N_MOD * d), lambda i, j: (i + 1, 0, 0)),
        ] + [_const_spec(c.shape) for c in consts],
        out_specs=pl.BlockSpec((1, tm, d), lambda i, j: (i, j, 0)),
        out_shape=jax.ShapeDtypeStruct((b, seq, d), F32),
        compiler_params=pltpu.CompilerParams(
            dimension_semantics=("arbitrary", "arbitrary"),
            vmem_limit_bytes=VMEM_LIMIT_BYTES),
        name="mixer_out_ffn",
    )(o, gb, x, mod3, *consts)


def kernel(x, c, ctx, c_ctx, w_mod, b_mod, norm_pre1, norm_post1, norm_pre2, norm_post2, w_in, hg_lb,
           hg_onorm, gla_w_gk, gla_b_gk, gla_onorm, w_br_hg, w_br_gla, w_out, w_ff_gate, w_ff_up, w_ff_down):
    b, seq, d = x.shape
    ctx_len = ctx.shape[1]
    assert w_mod.shape[0] == 1 and d == 2 * BRANCH_W
    assert seq % CHUNK == 0 and ctx_len % CHUNK == 0
    tm = 256

    c_all = jnp.zeros((MOD_ROWS, d), F32).at[0].set(c_ctx).at[1:b + 1].set(c)
    mod3 = _modulation(c_all, w_mod[0], b_mod[0][None]).reshape(MOD_ROWS, 1, N_MOD * d)

    w = w_in[0]
    n9 = 9 * BRANCH_W
    lr_w = 2 * GLA_RANK
    w_in_r = jnp.concatenate(
        [w[:, :n9], w[:, n9 + lr_w:], w[:, n9:n9 + lr_w], jnp.zeros((d, LR_PAD - lr_w), F32)],
        axis=1).astype(BF16)
    wgk = jnp.zeros((LR_PAD, 2 * BRANCH_W), F32)
    wgk = wgk.at[:GLA_RANK, :BRANCH_W].set(gla_w_gk[0, 0]).at[GLA_RANK:lr_w, BRANCH_W:].set(gla_w_gk[0, 1])
    bgk = gla_b_gk[0].reshape(1, 2 * BRANCH_W)

    sb, sf, gb = _in_projection(x, ctx, mod3, norm_pre1[0][None], w_in_r, hg_lb, wgk.astype(BF16), bgk, tm=tm)
    o = _bidirectional_scan(sb, sf, ctx_len=ctx_len, seq=seq)
    return _tail(o, gb, x, mod3, hg_onorm[0][None], gla_onorm[0][None], norm_post1[0][None],
                 norm_pre2[0][None], norm_post2[0][None],
                 w_br_hg[0].astype(BF16), w_br_gla[0].astype(BF16), w_out[0].astype(BF16),
                 w_ff_gate[0].astype(BF16), w_ff_up[0].astype(BF16), w_ff_down[0].astype(BF16),
                 tm=tm, ctx_len=ctx_len)
```

```python
import functools

import jax
import jax.numpy as jnp
from jax import lax
from jax.experimental import pallas as pl
from jax.experimental.pallas import tpu as pltpu

F32 = jnp.float32
BF16 = jnp.bfloat16

EPS = 1e-6
LOG2E = 1.4426950408889634
CHUNK = 64
N_MOD = 6
HEADS = 4
HEAD_DIM = 128
BRANCH_W = HEADS * HEAD_DIM
DIR_W = 2 * BRANCH_W
GLA_RANK = 16
GLA_GATE_NORM = 16.0
LR_PAD = 128
MOD_ROWS = 16
VMEM_LIMIT_BYTES = 56 * 1024 * 1024


def _dot(a, b):
    return jnp.dot(a, b, preferred_element_type=F32)


def _dot_nt(a, b):
    return lax.dot_general(a, b, (((1,), (1,)), ((), ())), preferred_element_type=F32)


def _dot_tn(a, b):
    return lax.dot_general(a, b, (((0,), (0,)), ((), ())), preferred_element_type=F32)


def _sigmoid(x):
    return 1.0 / (1.0 + jnp.exp(-x))


def _silu(x):
    return x * _sigmoid(x)


def _rms(x):
    return x * lax.rsqrt(jnp.mean(x * x, axis=-1, keepdims=True) + EPS)


def _const_spec(shape):
    nd = len(shape)
    return pl.BlockSpec(shape, lambda *_: (0,) * nd, pipeline_mode=pl.Buffered(1))


def _mod_kernel(c_ref, w_ref, b_ref, o_ref):
    a = _silu(c_ref[...]).astype(BF16)
    o_ref[...] = _dot(a, w_ref[...].astype(BF16)) + b_ref[...]


def _modulation(c_all, w_mod, b_mod):
    d = c_all.shape[1]
    n = w_mod.shape[1]
    blk = 1024
    return pl.pallas_call(
        _mod_kernel,
        grid=(n // blk,),
        in_specs=[pl.BlockSpec((MOD_ROWS, d), lambda j: (0, 0)),
                  pl.BlockSpec((d, blk), lambda j: (0, j)),
                  pl.BlockSpec((1, blk), lambda j: (0, j))],
        out_specs=pl.BlockSpec((MOD_ROWS, blk), lambda j: (0, j)),
        out_shape=jax.ShapeDtypeStruct((MOD_ROWS, n), F32),
        name="modulation",
    )(c_all, w_mod, b_mod)


def _inproj_kernel(x_ref, ctx_ref, mod_ref, pre_ref, w_ref, lb_ref, wgk_ref, bgk_ref,
                   qd_ref, kd_ref, v_ref, el_ref, oi_ref, gb_ref, *, d_model, tm):
    j = pl.program_id(1)
    z = jnp.where(j == 0, ctx_ref[0], x_ref[0])
    mod = mod_ref[0]
    shift, scale = mod[:, :d_model], mod[:, d_model:2 * d_model]
    h = (_rms(z) * pre_ref[...] * (1.0 + scale) + shift).astype(BF16)

    lbr = lb_ref[...]
    e = jnp.exp(lbr - jnp.max(lbr, axis=0, keepdims=True))
    lb = e[0] / jnp.sum(e, axis=0)

    n_ch = tm // CHUNK
    row = lax.broadcasted_iota(jnp.int32, (tm, tm), 0)
    col = lax.broadcasted_iota(jnp.int32, (tm, tm), 1)
    shift_bits = CHUNK.bit_length() - 1
    same = jnp.right_shift(row, shift_bits) == jnp.right_shift(col, shift_bits)
    masks = (same & (col <= row), same & (col >= row))
    tris = tuple(jnp.where(m, 1.0, 0.0).astype(BF16) for m in masks)
    mids = (CHUNK // 2, CHUNK // 2 - 1)
    lasts = (CHUNK - 1, 0)

    def proj(g):
        return _dot(h, w_ref[:, g * BRANCH_W:(g + 1) * BRANCH_W])

    def direction(q, k, lf, d, col0):
        lf2 = lf * LOG2E
        hi = lf2.astype(BF16)
        md = (lf2 - hi.astype(F32)).astype(BF16)
        c = _dot(tris[d], hi) + _dot(tris[d], md)
        qts, kts = [], []
        for n in range(n_ch):
            rows = slice(n * CHUNK, (n + 1) * CHUNK)
            cj = c[rows]
            r = cj[mids[d]:mids[d] + 1]
            cl = cj[lasts[d]:lasts[d] + 1]
            dq = cj - r
            qt = q[rows] * jnp.exp2(dq)
            kt = k[rows] * jnp.exp2(-dq)
            qd_ref[0, rows, col0:col0 + BRANCH_W] = (qt * jnp.exp2(r)).astype(BF16)
            kd_ref[0, rows, col0:col0 + BRANCH_W] = (kt * jnp.exp2(cl - r)).astype(BF16)
            el_ref[0, n, :, col0:col0 + BRANCH_W] = jnp.exp2(cl)
            qts.append(qt.astype(BF16))
            kts.append(kt.astype(BF16))
        return jnp.concatenate(qts, axis=0), jnp.concatenate(kts, axis=0)

    def intra(qt, kt, v, d):
        outs = []
        for hd in range(HEADS):
            sl = slice(hd * HEAD_DIM, (hd + 1) * HEAD_DIM)
            a = _dot_nt(qt[:, sl], kt[:, sl])
            a = jnp.where(masks[d], a, 0.0).astype(BF16)
            outs.append(_dot(a, v[:, sl]))
        return jnp.concatenate(outs, axis=-1)

    q = _silu(proj(0))
    v = proj(1).astype(BF16)
    v_ref[0, :, 0:BRANCH_W] = v
    o = None
    for d in range(2):
        raw = proj(2 + d)
        lbd = lb[d:d + 1]
        f = lbd + (1.0 - lbd) * _sigmoid(raw)
        qt, kt = direction(q, 1.0 - f, jnp.log(f), d, d * DIR_W)
        od = intra(qt, kt, v, d)
        o = od if o is None else o + od
    oi_ref[0, :, 0:BRANCH_W] = o.astype(BF16)
    gb_ref[0, :, 0:BRANCH_W] = _silu(proj(4)).astype(BF16)

    q = proj(5) * HEAD_DIM ** -0.5
    k = proj(6)
    v = proj(7).astype(BF16)
    v_ref[0, :, BRANCH_W:2 * BRANCH_W] = v
    lr = _dot(h, w_ref[:, 13 * BRANCH_W:13 * BRANCH_W + LR_PAD]).astype(BF16)
    xg = _dot(lr, wgk_ref[...]) + bgk_ref[...]
    ls = (jnp.minimum(xg, 0.0) - jnp.log(1.0 + jnp.exp(-jnp.abs(xg)))) * (1.0 / GLA_GATE_NORM)
    o = None
    for d in range(2):
        qt, kt = direction(q, k, ls[:, d * BRANCH_W:(d + 1) * BRANCH_W], d, d * DIR_W + BRANCH_W)
        od = intra(qt, kt, v, d)
        o = od if o is None else o + od
    oi_ref[0, :, BRANCH_W:2 * BRANCH_W] = o.astype(BF16)
    gb_ref[0, :, BRANCH_W:2 * BRANCH_W] = _silu(proj(8)).astype(BF16)

    for g in range(9, 13):
        c0 = 2 * BRANCH_W + (g - 9) * BRANCH_W
        gb_ref[0, :, c0:c0 + BRANCH_W] = _sigmoid(proj(g)).astype(BF16)


def _in_projection(x, ctx, mod3, norm_pre, w_in_r, hg_lb, wgk, bgk, *, tm):
    b, seq, d = x.shape
    ctx_len = ctx.shape[1]
    assert ctx_len == tm and tm % CHUNK == 0
    t_all = ctx_len + seq
    n_lat = seq // tm
    n_ch = tm // CHUNK
    nw = w_in_r.shape[1]
    kern = functools.partial(_inproj_kernel, d_model=d, tm=tm)

    def rows(w):
        return pl.BlockSpec((1, tm, w), lambda i, j: (i, j, 0))

    return pl.pallas_call(
        kern,
        grid=(b, n_lat + 1),
        in_specs=[
            pl.BlockSpec((1, tm, d), lambda i, j: (i, jnp.maximum(j - 1, 0), 0)),
            pl.BlockSpec((1, tm, d), lambda i, j: (i, 0, 0)),
            pl.BlockSpec((1, 1, N_MOD * d), lambda i, j: (jnp.where(j == 0, 0, i + 1), 0, 0)),
            _const_spec((1, d)),
            _const_spec((d, nw)),
            _const_spec(hg_lb.shape),
            _const_spec(wgk.shape),
            _const_spec(bgk.shape),
        ],
        out_specs=[
            rows(2 * DIR_W),
            rows(2 * DIR_W),
            rows(DIR_W),
            pl.BlockSpec((1, n_ch, 1, 2 * DIR_W), lambda i, j: (i, j, 0, 0)),
            rows(DIR_W),
            rows(6 * BRANCH_W),
        ],
        out_shape=[
            jax.ShapeDtypeStruct((b, t_all, 2 * DIR_W), BF16),
            jax.ShapeDtypeStruct((b, t_all, 2 * DIR_W), BF16),
            jax.ShapeDtypeStruct((b, t_all, DIR_W), BF16),
            jax.ShapeDtypeStruct((b, t_all // CHUNK, 1, 2 * DIR_W), F32),
            jax.ShapeDtypeStruct((b, t_all, DIR_W), BF16),
            jax.ShapeDtypeStruct((b, t_all, 6 * BRANCH_W), BF16),
        ],
        compiler_params=pltpu.CompilerParams(
            dimension_semantics=("arbitrary", "arbitrary"),
            vmem_limit_bytes=VMEM_LIMIT_BYTES),
        name="in_projection",
    )(x, ctx, mod3, norm_pre, w_in_r, hg_lb, wgk, bgk)


def _scan_kernel(qdf_ref, qdb_ref, kdf_ref, kdb_ref, vf_ref, vb_ref, elf_ref, elb_ref,
                 o_ref, s_ref, *, ctx_chunks, n_chunks):
    s = pl.program_id(1)

    @pl.when(s == 0)
    def _():
        s_ref[...] = jnp.zeros_like(s_ref)
        o_ref[...] = jnp.zeros_like(o_ref)

    def run(with_out):
        res = []
        for d, (qd_ref, kd_ref, v_ref, el_ref) in enumerate((
                (qdf_ref, kdf_ref, vf_ref, elf_ref), (qdb_ref, kdb_ref, vb_ref, elb_ref))):
            e_last = el_ref[0, 0]
            outs = []
            for u in range(2 * HEADS):
                sl = slice(u * HEAD_DIM, (u + 1) * HEAD_DIM)
                st = s_ref[d * 2 * HEADS + u]
                if with_out:
                    outs.append(_dot_nt(qd_ref[0, :, sl], st.astype(BF16)))
                s_ref[d * 2 * HEADS + u] = st * e_last[:, sl] + _dot_tn(v_ref[0, :, sl], kd_ref[0, :, sl])
            res.append(jnp.concatenate(outs, axis=-1) if with_out else None)
        return res

    @pl.when(s < ctx_chunks)
    def _():
        run(False)

    @pl.when(s >= ctx_chunks)
    def _():
        o_f, o_b = run(True)
        rf = pl.multiple_of((s - ctx_chunks) * CHUNK, CHUNK)
        rb = pl.multiple_of((n_chunks - 1 - s) * CHUNK, CHUNK)
        o_ref[0, pl.ds(rf, CHUNK), :] += o_f
        o_ref[0, pl.ds(rb, CHUNK), :] += o_b


def _state_scan(qd, kd, v, el, *, ctx_len, seq):
    b = qd.shape[0]
    ctx_chunks = ctx_len // CHUNK
    n_chunks = (ctx_len + seq) // CHUNK

    def bw_chunk(s):
        return jnp.where(s < ctx_chunks, ctx_chunks - 1 - s, n_chunks + ctx_chunks - 1 - s)

    def fw_rows(i, s):
        return (i, s, 0)

    def bw_rows(i, s):
        return (i, bw_chunk(s), 1)

    def bw_rows_v(i, s):
        return (i, bw_chunk(s), 0)

    kern = functools.partial(_scan_kernel, ctx_chunks=ctx_chunks, n_chunks=n_chunks)
    blk = (1, CHUNK, DIR_W)
    eblk = (1, 1, 1, DIR_W)
    return pl.pallas_call(
        kern,
        grid=(b, n_chunks),
        in_specs=[pl.BlockSpec(blk, fw_rows), pl.BlockSpec(blk, bw_rows),
                  pl.BlockSpec(blk, fw_rows), pl.BlockSpec(blk, bw_rows),
                  pl.BlockSpec(blk, fw_rows), pl.BlockSpec(blk, bw_rows_v),
                  pl.BlockSpec(eblk, lambda i, s: (i, s, 0, 0)),
                  pl.BlockSpec(eblk, lambda i, s: (i, bw_chunk(s), 0, 1))],
        out_specs=pl.BlockSpec((1, seq, DIR_W), lambda i, s: (i, 0, 0)),
        out_shape=jax.ShapeDtypeStruct((b, seq, DIR_W), F32),
        scratch_shapes=[pltpu.VMEM((4 * HEADS, HEAD_DIM, HEAD_DIM), F32)],
        compiler_params=pltpu.CompilerParams(
            dimension_semantics=("arbitrary", "arbitrary"),
            vmem_limit_bytes=VMEM_LIMIT_BYTES),
        name="state_scan",
    )(qd, qd, kd, kd, v, v, el, el)


def _tail_kernel(oi_ref, os_ref, gb_ref, x_ref, mod_ref, hgn_ref, gln_ref, post1_ref, pre2_ref, post2_ref,
                 wbh_ref, wbg_ref, wout_ref, wg_ref, wu_ref, wd_ref, out_ref, *, d_model):
    o = oi_ref[0].astype(F32) + os_ref[0]
    parts = []
    for h in range(2 * HEADS):
        oh = o[:, h * HEAD_DIM:(h + 1) * HEAD_DIM]
        w = hgn_ref[...] if h < HEADS else gln_ref[...]
        parts.append(_rms(oh) * w)
    on = jnp.concatenate(parts, axis=-1)
    og = (on * gb_ref[0, :, 0:2 * BRANCH_W].astype(F32)).astype(BF16)
    yh = _dot(og[:, :BRANCH_W], wbh_ref[...])
    yg = _dot(og[:, BRANCH_W:], wbg_ref[...])
    c0 = 2 * BRANCH_W
    merged = (gb_ref[0, :, c0:c0 + d_model].astype(F32) * yh
              + gb_ref[0, :, c0 + d_model:c0 + 2 * d_model].astype(F32) * yg)
    y = _dot(merged.astype(BF16), wout_ref[...])

    mod = mod_ref[0]
    m = [mod[:, i * d_model:(i + 1) * d_model] for i in range(N_MOD)]
    z1 = x_ref[0] + _rms(y) * post1_ref[...] * m[2]
    h2 = (_rms(z1) * pre2_ref[...] * (1.0 + m[4]) + m[3]).astype(BF16)
    g = _dot(h2, wg_ref[...])
    u = _dot(h2, wu_ref[...])
    a = (_silu(g) * u).astype(BF16)
    y2 = _dot(a, wd_ref[...])
    out_ref[0] = z1 + _rms(y2) * post2_ref[...] * m[5]


def _tail(oi, o_scan, gb, x, mod3, hgn, gln, post1, pre2, post2, wbh, wbg, wout, wg, wu, wd, *, tm, ctx_len):
    b, seq, d = x.shape
    off = ctx_len // tm
    kern = functools.partial(_tail_kernel, d_model=d)
    consts = (hgn, gln, post1, pre2, post2, wbh, wbg, wout, wg, wu, wd)
    return pl.pallas_call(
        kern,
        grid=(b, seq // tm),
        in_specs=[
            pl.BlockSpec((1, tm, DIR_W), lambda i, j: (i, j + off, 0)),
            pl.BlockSpec((1, tm, DIR_W), lambda i, j: (i, j, 0)),
            pl.BlockSpec((1, tm, 6 * BRANCH_W), lambda i, j: (i, j + off, 0)),
            pl.BlockSpec((1, tm, d), lambda i, j: (i, j, 0)),
            pl.BlockSpec((1, 1, N_MOD * d), lambda i, j: (i + 1, 0, 0)),
        ] + [_const_spec(c.shape) for c in consts],
        out_specs=pl.BlockSpec((1, tm, d), lambda i, j: (i, j, 0)),
        out_shape=jax.ShapeDtypeStruct((b, seq, d), F32),
        compiler_params=pltpu.CompilerParams(
            dimension_semantics=("arbitrary", "arbitrary"),
            vmem_limit_bytes=VMEM_LIMIT_BYTES),
        name="mixer_out_ffn",
    )(oi, o_scan, gb, x, mod3, *consts)


def kernel(x, c, ctx, c_ctx, w_mod, b_mod, norm_pre1, norm_post1, norm_pre2, norm_post2, w_in, hg_lb,
           hg_onorm, gla_w_gk, gla_b_gk, gla_onorm, w_br_hg, w_br_gla, w_out, w_ff_gate, w_ff_up, w_ff_down):
    b, seq, d = x.shape
    ctx_len = ctx.shape[1]
    assert w_mod.shape[0] == 1 and d == DIR_W
    assert seq % CHUNK == 0 and ctx_len % CHUNK == 0
    tm = 256

    c_all = jnp.zeros((MOD_ROWS, d), F32).at[0].set(c_ctx).at[1:b + 1].set(c)
    mod3 = _modulation(c_all, w_mod[0], b_mod[0][None]).reshape(MOD_ROWS, 1, N_MOD * d)

    w = w_in[0]
    n9 = 9 * BRANCH_W
    lr_w = 2 * GLA_RANK
    w_in_r = jnp.concatenate(
        [w[:, :n9], w[:, n9 + lr_w:], w[:, n9:n9 + lr_w], jnp.zeros((d, LR_PAD - lr_w), F32)],
        axis=1).astype(BF16)
    wgk = jnp.zeros((LR_PAD, 2 * BRANCH_W), F32)
    wgk = wgk.at[:GLA_RANK, :BRANCH_W].set(gla_w_gk[0, 0]).at[GLA_RANK:lr_w, BRANCH_W:].set(gla_w_gk[0, 1])
    bgk = gla_b_gk[0].reshape(1, 2 * BRANCH_W)

    qd, kd, v, el, oi, gb = _in_projection(
        x, ctx, mod3, norm_pre1[0][None], w_in_r, hg_lb, wgk.astype(BF16), bgk, tm=tm)
    o_scan = _state_scan(qd, kd, v, el, ctx_len=ctx_len, seq=seq)
    return _tail(oi, o_scan, gb, x, mod3, hg_onorm[0][None], gla_onorm[0][None], norm_post1[0][None],
                 norm_pre2[0][None], norm_post2[0][None],
                 w_br_hg[0].astype(BF16), w_br_gla[0].astype(BF16), w_out[0].astype(BF16),
                 w_ff_gate[0].astype(BF16), w_ff_up[0].astype(BF16), w_ff_down[0].astype(BF16),
                 tm=tm, ctx_len=ctx_len)
```

```python
import functools

import jax
import jax.numpy as jnp
from jax import lax
from jax.experimental import pallas as pl
from jax.experimental.pallas import tpu as pltpu

F32 = jnp.float32
BF16 = jnp.bfloat16

EPS = 1e-6
LOG2E = 1.4426950408889634
CHUNK = 64
N_MOD = 6
HEADS = 4
HEAD_DIM = 128
BRANCH_W = HEADS * HEAD_DIM
DIR_W = 2 * BRANCH_W
GLA_RANK = 16
GLA_GATE_NORM = 16.0
LR_PAD = 128
G_HQ, G_HI, G_HF, G_HB, G_HGATE, G_GQ, G_GK, G_GV, G_GGATE = range(9)
LR_COL = 9 * BRANCH_W
MOD_ROWS = 16
VMEM_LIMIT_BYTES = 56 * 1024 * 1024


def _dot(a, b):
    return jnp.dot(a, b, preferred_element_type=F32)


def _dot_nt(a, b):
    return lax.dot_general(a, b, (((1,), (1,)), ((), ())), preferred_element_type=F32)


def _dot_tn(a, b):
    return lax.dot_general(a, b, (((0,), (0,)), ((), ())), preferred_element_type=F32)


def _sigmoid(x):
    return 1.0 / (1.0 + jnp.exp(-x))


def _silu(x):
    return x * _sigmoid(x)


def _rms(x):
    return x * lax.rsqrt(jnp.mean(x * x, axis=-1, keepdims=True) + EPS)


def _const_spec(shape):
    nd = len(shape)
    return pl.BlockSpec(shape, lambda *_: (0,) * nd, pipeline_mode=pl.Buffered(1))


def _mod_kernel(c_ref, w_ref, b_ref, o_ref):
    a = _silu(c_ref[...]).astype(BF16)
    o_ref[...] = _dot(a, w_ref[...].astype(BF16)) + b_ref[...]


def _modulation(c_all, w_mod, b_mod):
    d = c_all.shape[1]
    n = w_mod.shape[1]
    blk = 1024
    return pl.pallas_call(
        _mod_kernel,
        grid=(n // blk,),
        in_specs=[pl.BlockSpec((MOD_ROWS, d), lambda j: (0, 0)),
                  pl.BlockSpec((d, blk), lambda j: (0, j)),
                  pl.BlockSpec((1, blk), lambda j: (0, j))],
        out_specs=pl.BlockSpec((MOD_ROWS, blk), lambda j: (0, j)),
        out_shape=jax.ShapeDtypeStruct((MOD_ROWS, n), F32),
        name="modulation",
    )(c_all, w_mod, b_mod)


def _inproj_kernel(x_ref, ctx_ref, mod_ref, pre_ref, w_ref, lb_ref, wgk_ref, bgk_ref,
                   qd_ref, kd_ref, v_ref, el_ref, oi_ref, *, d_model, tm):
    j = pl.program_id(1)
    z = jnp.where(j == 0, ctx_ref[0], x_ref[0])
    mod = mod_ref[0]
    shift, scale = mod[:, :d_model], mod[:, d_model:2 * d_model]
    h = (_rms(z) * pre_ref[...] * (1.0 + scale) + shift).astype(BF16)

    lbr = lb_ref[...]
    e = jnp.exp(lbr - jnp.max(lbr, axis=0, keepdims=True))
    lb = e[0] / jnp.sum(e, axis=0)

    n_ch = tm // CHUNK
    row = lax.broadcasted_iota(jnp.int32, (tm, tm), 0)
    col = lax.broadcasted_iota(jnp.int32, (tm, tm), 1)
    shift_bits = CHUNK.bit_length() - 1
    same = jnp.right_shift(row, shift_bits) == jnp.right_shift(col, shift_bits)
    masks = (same & (col <= row), same & (col >= row))
    tris = tuple(jnp.where(m, 1.0, 0.0).astype(BF16) for m in masks)
    mids = (CHUNK // 2, CHUNK // 2 - 1)
    lasts = (CHUNK - 1, 0)

    def proj(g):
        return _dot(h, w_ref[:, g * BRANCH_W:(g + 1) * BRANCH_W])

    def direction(q, k, lf, d, col0):
        lf2 = lf * LOG2E
        hi = lf2.astype(BF16)
        md = (lf2 - hi.astype(F32)).astype(BF16)
        c = _dot(tris[d], hi) + _dot(tris[d], md)
        qts, kts = [], []
        for n in range(n_ch):
            rows = slice(n * CHUNK, (n + 1) * CHUNK)
            cj = c[rows]
            r = cj[mids[d]:mids[d] + 1]
            cl = cj[lasts[d]:lasts[d] + 1]
            dq = cj - r
            qt = q[rows] * jnp.exp2(dq)
            kt = k[rows] * jnp.exp2(-dq)
            qd_ref[0, rows, col0:col0 + BRANCH_W] = (qt * jnp.exp2(r)).astype(BF16)
            kd_ref[0, rows, col0:col0 + BRANCH_W] = (kt * jnp.exp2(cl - r)).astype(BF16)
            el_ref[0, n, :, col0:col0 + BRANCH_W] = jnp.exp2(cl)
            qts.append(qt.astype(BF16))
            kts.append(kt.astype(BF16))
        return jnp.concatenate(qts, axis=0), jnp.concatenate(kts, axis=0)

    def intra(qt, kt, v, d):
        outs = []
        for hd in range(HEADS):
            sl = slice(hd * HEAD_DIM, (hd + 1) * HEAD_DIM)
            a = _dot_nt(qt[:, sl], kt[:, sl])
            a = jnp.where(masks[d], a, 0.0).astype(BF16)
            outs.append(_dot(a, v[:, sl]))
        return jnp.concatenate(outs, axis=-1)

    q = _silu(proj(G_HQ))
    v = proj(G_HI).astype(BF16)
    v_ref[0, :, 0:BRANCH_W] = v
    o = None
    for d in range(2):
        raw = proj(G_HF + d)
        lbd = lb[d:d + 1]
        f = lbd + (1.0 - lbd) * _sigmoid(raw)
        qt, kt = direction(q, 1.0 - f, jnp.log(f), d, d * DIR_W)
        od = intra(qt, kt, v, d)
        o = od if o is None else o + od
    oi_ref[0, :, 0:BRANCH_W] = o.astype(BF16)

    q = proj(G_GQ) * HEAD_DIM ** -0.5
    k = proj(G_GK)
    v = proj(G_GV).astype(BF16)
    v_ref[0, :, BRANCH_W:2 * BRANCH_W] = v
    lr = _dot(h, w_ref[:, LR_COL:LR_COL + LR_PAD]).astype(BF16)
    xg = _dot(lr, wgk_ref[...]) + bgk_ref[...]
    ls = (jnp.minimum(xg, 0.0) - jnp.log(1.0 + jnp.exp(-jnp.abs(xg)))) * (1.0 / GLA_GATE_NORM)
    o = None
    for d in range(2):
        qt, kt = direction(q, k, ls[:, d * BRANCH_W:(d + 1) * BRANCH_W], d, d * DIR_W + BRANCH_W)
        od = intra(qt, kt, v, d)
        o = od if o is None else o + od
    oi_ref[0, :, BRANCH_W:2 * BRANCH_W] = o.astype(BF16)


def _in_projection(x, ctx, mod3, norm_pre, w_in_r, hg_lb, wgk, bgk, *, tm):
    b, seq, d = x.shape
    ctx_len = ctx.shape[1]
    assert ctx_len == tm and tm % CHUNK == 0
    t_all = ctx_len + seq
    n_lat = seq // tm
    n_ch = tm // CHUNK
    nw = w_in_r.shape[1]
    kern = functools.partial(_inproj_kernel, d_model=d, tm=tm)

    def rows(w):
        return pl.BlockSpec((1, tm, w), lambda i, j: (i, j, 0))

    return pl.pallas_call(
        kern,
        grid=(b, n_lat + 1),
        in_specs=[
            pl.BlockSpec((1, tm, d), lambda i, j: (i, jnp.maximum(j - 1, 0), 0)),
            pl.BlockSpec((1, tm, d), lambda i, j: (i, 0, 0)),
            pl.BlockSpec((1, 1, N_MOD * d), lambda i, j: (jnp.where(j == 0, 0, i + 1), 0, 0)),
            _const_spec((1, d)),
            _const_spec((d, nw)),
            _const_spec(hg_lb.shape),
            _const_spec(wgk.shape),
            _const_spec(bgk.shape),
        ],
        out_specs=[
            rows(2 * DIR_W),
            rows(2 * DIR_W),
            rows(DIR_W),
            pl.BlockSpec((1, n_ch, 1, 2 * DIR_W), lambda i, j: (i, j, 0, 0)),
            rows(DIR_W),
        ],
        out_shape=[
            jax.ShapeDtypeStruct((b, t_all, 2 * DIR_W), BF16),
            jax.ShapeDtypeStruct((b, t_all, 2 * DIR_W), BF16),
            jax.ShapeDtypeStruct((b, t_all, DIR_W), BF16),
            jax.ShapeDtypeStruct((b, t_all // CHUNK, 1, 2 * DIR_W), F32),
            jax.ShapeDtypeStruct((b, t_all, DIR_W), BF16),
        ],
        compiler_params=pltpu.CompilerParams(
            dimension_semantics=("arbitrary", "arbitrary"),
            vmem_limit_bytes=VMEM_LIMIT_BYTES),
        name="in_projection",
    )(x, ctx, mod3, norm_pre, w_in_r, hg_lb, wgk, bgk)


def _scan_kernel(qdf_ref, qdb_ref, kdf_ref, kdb_ref, vf_ref, vb_ref, elf_ref, elb_ref,
                 o_ref, s_ref, *, ctx_steps, n_steps, group):
    s = pl.program_id(1)
    rows_per_step = group * CHUNK

    @pl.when(s == 0)
    def _():
        s_ref[...] = jnp.zeros_like(s_ref)
        o_ref[...] = jnp.zeros_like(o_ref)

    def run(with_out):
        if with_out:
            base = (pl.multiple_of((s - ctx_steps) * rows_per_step, rows_per_step),
                    pl.multiple_of((n_steps - 1 - s) * rows_per_step, rows_per_step))
        for g in range(group):
            for d, (qd_ref, kd_ref, v_ref, el_ref) in enumerate((
                    (qdf_ref, kdf_ref, vf_ref, elf_ref), (qdb_ref, kdb_ref, vb_ref, elb_ref))):
                ci = g if d == 0 else group - 1 - g
                rows = slice(ci * CHUNK, (ci + 1) * CHUNK)
                e_last = el_ref[0, ci]
                outs = []
                for u in range(2 * HEADS):
                    sl = slice(u * HEAD_DIM, (u + 1) * HEAD_DIM)
                    st = s_ref[d * 2 * HEADS + u]
                    if with_out:
                        outs.append(_dot_nt(qd_ref[0, rows, sl], st.astype(BF16)))
                    s_ref[d * 2 * HEADS + u] = (st * e_last[:, sl]
                                                + _dot_tn(v_ref[0, rows, sl], kd_ref[0, rows, sl]))
                if with_out:
                    o_ref[0, pl.ds(base[d] + ci * CHUNK, CHUNK), :] += jnp.concatenate(outs, axis=-1)

    @pl.when(s < ctx_steps)
    def _():
        run(False)

    @pl.when(s >= ctx_steps)
    def _():
        run(True)


def _state_scan(qd, kd, v, el, *, ctx_len, seq, group):
    b = qd.shape[0]
    rows_per_step = group * CHUNK
    assert ctx_len % rows_per_step == 0 and seq % rows_per_step == 0
    ctx_steps = ctx_len // rows_per_step
    n_steps = (ctx_len + seq) // rows_per_step

    def bw_block(s):
        return jnp.where(s < ctx_steps, ctx_steps - 1 - s, n_steps + ctx_steps - 1 - s)

    def fw_rows(i, s):
        return (i, s, 0)

    def bw_rows(i, s):
        return (i, bw_block(s), 1)

    def bw_rows_v(i, s):
        return (i, bw_block(s), 0)

    kern = functools.partial(_scan_kernel, ctx_steps=ctx_steps, n_steps=n_steps, group=group)
    blk = (1, rows_per_step, DIR_W)
    eblk = (1, group, 1, DIR_W)
    return pl.pallas_call(
        kern,
        grid=(b, n_steps),
        in_specs=[pl.BlockSpec(blk, fw_rows), pl.BlockSpec(blk, bw_rows),
                  pl.BlockSpec(blk, fw_rows), pl.BlockSpec(blk, bw_rows),
                  pl.BlockSpec(blk, fw_rows), pl.BlockSpec(blk, bw_rows_v),
                  pl.BlockSpec(eblk, lambda i, s: (i, s, 0, 0)),
                  pl.BlockSpec(eblk, lambda i, s: (i, bw_block(s), 0, 1))],
        out_specs=pl.BlockSpec((1, seq, DIR_W), lambda i, s: (i, 0, 0)),
        out_shape=jax.ShapeDtypeStruct((b, seq, DIR_W), F32),
        scratch_shapes=[pltpu.VMEM((4 * HEADS, HEAD_DIM, HEAD_DIM), F32)],
        compiler_params=pltpu.CompilerParams(
            dimension_semantics=("arbitrary", "arbitrary"),
            vmem_limit_bytes=VMEM_LIMIT_BYTES),
        name="state_scan",
    )(qd, qd, kd, kd, v, v, el, el)


def _tail_kernel(oi_ref, os_ref, x_ref, mod_ref, hgn_ref, gln_ref, pre1_ref, post1_ref, pre2_ref, post2_ref,
                 wgh_ref, wgg_ref, wgm_ref, wbh_ref, wbg_ref, wout_ref, wg_ref, wu_ref, wd_ref, out_ref,
                 *, d_model):
    x = x_ref[0]
    mod = mod_ref[0]
    m = [mod[:, i * d_model:(i + 1) * d_model] for i in range(N_MOD)]
    h1 = (_rms(x) * pre1_ref[...] * (1.0 + m[1]) + m[0]).astype(BF16)

    o = oi_ref[0].astype(F32) + os_ref[0]
    gates = (wgh_ref, wgg_ref)
    norms = (hgn_ref, gln_ref)
    ys = []
    for br, wb_ref in enumerate((wbh_ref, wbg_ref)):
        parts = []
        for hd in range(HEADS):
            c0 = br * BRANCH_W + hd * HEAD_DIM
            parts.append(_rms(o[:, c0:c0 + HEAD_DIM]) * norms[br][...])
        on = jnp.concatenate(parts, axis=-1)
        og = (on * _silu(_dot(h1, gates[br][...]))).astype(BF16)
        ys.append(_dot(og, wb_ref[...]))
    gm = _sigmoid(_dot(h1, wgm_ref[...]))
    merged = gm[:, :d_model] * ys[0] + gm[:, d_model:] * ys[1]
    y = _dot(merged.astype(BF16), wout_ref[...])

    z1 = x + _rms(y) * post1_ref[...] * m[2]
    h2 = (_rms(z1) * pre2_ref[...] * (1.0 + m[4]) + m[3]).astype(BF16)
    g = _dot(h2, wg_ref[...])
    u = _dot(h2, wu_ref[...])
    a = (_silu(g) * u).astype(BF16)
    y2 = _dot(a, wd_ref[...])
    out_ref[0] = z1 + _rms(y2) * post2_ref[...] * m[5]


def _tail(oi, o_scan, x, mod3, *consts, tm, ctx_len):
    b, seq, d = x.shape
    off = ctx_len // tm
    kern = functools.partial(_tail_kernel, d_model=d)
    return pl.pallas_call(
        kern,
        grid=(b, seq // tm),
        in_specs=[
            pl.BlockSpec((1, tm, DIR_W), lambda i, j: (i, j + off, 0)),
            pl.BlockSpec((1, tm, DIR_W), lambda i, j: (i, j, 0)),
            pl.BlockSpec((1, tm, d), lambda i, j: (i, j, 0)),
            pl.BlockSpec((1, 1, N_MOD * d), lambda i, j: (i + 1, 0, 0)),
        ] + [_const_spec(c.shape) for c in consts],
        out_specs=pl.BlockSpec((1, tm, d), lambda i, j: (i, j, 0)),
        out_shape=jax.ShapeDtypeStruct((b, seq, d), F32),
        compiler_params=pltpu.CompilerParams(
            dimension_semantics=("arbitrary", "arbitrary"),
            vmem_limit_bytes=VMEM_LIMIT_BYTES),
        name="mixer_out_ffn",
    )(oi, o_scan, x, mod3, *consts)


def kernel(x, c, ctx, c_ctx, w_mod, b_mod, norm_pre1, norm_post1, norm_pre2, norm_post2, w_in, hg_lb,
           hg_onorm, gla_w_gk, gla_b_gk, gla_onorm, w_br_hg, w_br_gla, w_out, w_ff_gate, w_ff_up, w_ff_down):
    b, seq, d = x.shape
    ctx_len = ctx.shape[1]
    assert w_mod.shape[0] == 1 and d == DIR_W
    assert seq % CHUNK == 0 and ctx_len % CHUNK == 0
    tm = 256

    c_all = jnp.zeros((MOD_ROWS, d), F32).at[0].set(c_ctx).at[1:b + 1].set(c)
    mod3 = _modulation(c_all, w_mod.reshape(d, N_MOD * d), b_mod).reshape(MOD_ROWS, 1, N_MOD * d)

    def bf(a):
        return a.reshape(a.shape[1:]).astype(BF16)

    w = bf(w_in)
    lr_w = 2 * GLA_RANK
    wgk = jnp.zeros((LR_PAD, 2 * BRANCH_W), F32)
    wgk = wgk.at[:GLA_RANK, :BRANCH_W].set(gla_w_gk[0, 0]).at[GLA_RANK:lr_w, BRANCH_W:].set(gla_w_gk[0, 1])
    bgk = gla_b_gk.reshape(1, 2 * BRANCH_W)

    qd, kd, v, el, oi = _in_projection(
        x, ctx, mod3, norm_pre1, w, hg_lb, wgk.astype(BF16), bgk, tm=tm)
    o_scan = _state_scan(qd, kd, v, el, ctx_len=ctx_len, seq=seq, group=tm // CHUNK)
    return _tail(oi, o_scan, x, mod3, hg_onorm, gla_onorm, norm_pre1, norm_post1, norm_pre2, norm_post2,
                 w[:, G_HGATE * BRANCH_W:(G_HGATE + 1) * BRANCH_W],
                 w[:, G_GGATE * BRANCH_W:(G_GGATE + 1) * BRANCH_W],
                 w[:, LR_COL + lr_w:],
                 bf(w_br_hg), bf(w_br_gla), bf(w_out), bf(w_ff_gate), bf(w_ff_up), bf(w_ff_down),
                 tm=tm, ctx_len=ctx_len)
```

```python
import functools

import jax
import jax.numpy as jnp
from jax import lax
from jax.experimental import pallas as pl
from jax.experimental.pallas import tpu as pltpu

F32 = jnp.float32
BF16 = jnp.bfloat16

EPS = 1e-6
LOG2E = 1.4426950408889634
CHUNK = 64
N_MOD = 6
HEADS = 4
HEAD_DIM = 128
BRANCH_W = HEADS * HEAD_DIM
DIR_W = 2 * BRANCH_W
GLA_RANK = 16
GLA_GATE_NORM = 16.0
LR_PAD = 128
G_HQ, G_HI, G_HF, G_HB, G_HGATE, G_GQ, G_GK, G_GV, G_GGATE = range(9)
LR_COL = 9 * BRANCH_W
MOD_ROWS = 16
VMEM_LIMIT_BYTES = 56 * 1024 * 1024
TAIL_ROWS = 512


def _dot(a, b):
    return jnp.dot(a, b, preferred_element_type=F32)


def _dot_nt(a, b):
    return lax.dot_general(a, b, (((1,), (1,)), ((), ())), preferred_element_type=F32)


def _dot_tn(a, b):
    return lax.dot_general(a, b, (((0,), (0,)), ((), ())), preferred_element_type=F32)


def _sigmoid(x):
    return 1.0 / (1.0 + jnp.exp(-x))


def _silu(x):
    return x * _sigmoid(x)


def _rms(x):
    return x * lax.rsqrt(jnp.mean(x * x, axis=-1, keepdims=True) + EPS)


def _const_spec(shape):
    nd = len(shape)
    return pl.BlockSpec(shape, lambda *_: (0,) * nd, pipeline_mode=pl.Buffered(1))


def _mod_kernel(c_ref, w_ref, b_ref, o_ref):
    a = _silu(c_ref[...]).astype(BF16)
    o_ref[...] = _dot(a, w_ref[...].astype(BF16)) + b_ref[...]


def _modulation(c_all, w_mod, b_mod):
    d = c_all.shape[1]
    n = w_mod.shape[1]
    blk = 1024
    return pl.pallas_call(
        _mod_kernel,
        grid=(n // blk,),
        in_specs=[pl.BlockSpec((MOD_ROWS, d), lambda j: (0, 0)),
                  pl.BlockSpec((d, blk), lambda j: (0, j)),
                  pl.BlockSpec((1, blk), lambda j: (0, j))],
        out_specs=pl.BlockSpec((MOD_ROWS, blk), lambda j: (0, j)),
        out_shape=jax.ShapeDtypeStruct((MOD_ROWS, n), F32),
        name="modulation",
    )(c_all, w_mod, b_mod)


def _inproj_kernel(x_ref, ctx_ref, mod_ref, pre_ref, w_ref, lb_ref, wgk_ref, bgk_ref,
                   qd_ref, kd_ref, v_ref, el_ref, oi_ref, *, d_model, tm):
    n_ch = tm // CHUNK
    mids = (CHUNK // 2, CHUNK // 2 - 1)
    lasts = (CHUNK - 1, 0)

    def body(z_ref, latent):
        mod = mod_ref[0]
        shift, scale = mod[:, :d_model], mod[:, d_model:2 * d_model]
        h = (_rms(z_ref[0]) * pre_ref[...] * (1.0 + scale) + shift).astype(BF16)

        lbr = lb_ref[...]
        e = jnp.exp(lbr - jnp.max(lbr, axis=0, keepdims=True))
        lb = e[0] / jnp.sum(e, axis=0)

        row = lax.broadcasted_iota(jnp.int32, (tm, tm), 0)
        col = lax.broadcasted_iota(jnp.int32, (tm, tm), 1)
        shift_bits = CHUNK.bit_length() - 1
        same = jnp.right_shift(row, shift_bits) == jnp.right_shift(col, shift_bits)
        masks = (same & (col <= row), same & (col >= row))
        tris = tuple(jnp.where(m, 1.0, 0.0).astype(BF16) for m in masks)

        def proj(g):
            return _dot(h, w_ref[:, g * BRANCH_W:(g + 1) * BRANCH_W])

        def direction(q, k, lf, d, col0):
            lf2 = lf * LOG2E
            hi = lf2.astype(BF16)
            md = (lf2 - hi.astype(F32)).astype(BF16)
            c = _dot(tris[d], hi) + _dot(tris[d], md)
            qts, kts = [], []
            for n in range(n_ch):
                rows = slice(n * CHUNK, (n + 1) * CHUNK)
                cj = c[rows]
                cl = cj[lasts[d]:lasts[d] + 1]
                el_ref[0, n, :, col0:col0 + BRANCH_W] = jnp.exp2(cl)
                if not latent:
                    kd_ref[0, rows, col0:col0 + BRANCH_W] = (k[rows] * jnp.exp2(cl - cj)).astype(BF16)
                    continue
                r = cj[mids[d]:mids[d] + 1]
                dq = cj - r
                qt = q[rows] * jnp.exp2(dq)
                kt = k[rows] * jnp.exp2(-dq)
                qd_ref[0, rows, col0:col0 + BRANCH_W] = (qt * jnp.exp2(r)).astype(BF16)
                kd_ref[0, rows, col0:col0 + BRANCH_W] = (kt * jnp.exp2(cl - r)).astype(BF16)
                qts.append(qt.astype(BF16))
                kts.append(kt.astype(BF16))
            if not latent:
                return None
            qt, kt = jnp.concatenate(qts, axis=0), jnp.concatenate(kts, axis=0)
            outs = []
            for hd in range(HEADS):
                sl = slice(hd * HEAD_DIM, (hd + 1) * HEAD_DIM)
                a = _dot_nt(qt[:, sl], kt[:, sl])
                a = jnp.where(masks[d], a, 0.0).astype(BF16)
                outs.append(_dot(a, v[:, sl]))
            return jnp.concatenate(outs, axis=-1)

        q = _silu(proj(G_HQ)) if latent else None
        v = proj(G_HI).astype(BF16)
        v_ref[0, :, 0:BRANCH_W] = v
        o = []
        for d in range(2):
            lbd = lb[d:d + 1]
            f = lbd + (1.0 - lbd) * _sigmoid(proj(G_HF + d))
            o.append(direction(q, 1.0 - f, jnp.log(f), d, d * DIR_W))
        if latent:
            oi_ref[0, :, 0:BRANCH_W] = (o[0] + o[1]).astype(BF16)

        q = proj(G_GQ) * HEAD_DIM ** -0.5 if latent else None
        k = proj(G_GK)
        v = proj(G_GV).astype(BF16)
        v_ref[0, :, BRANCH_W:2 * BRANCH_W] = v
        lr = _dot(h, w_ref[:, LR_COL:LR_COL + LR_PAD]).astype(BF16)
        xg = _dot(lr, wgk_ref[...]) + bgk_ref[...]
        ls = (jnp.minimum(xg, 0.0) - jnp.log(1.0 + jnp.exp(-jnp.abs(xg)))) * (1.0 / GLA_GATE_NORM)
        o = []
        for d in range(2):
            o.append(direction(q, k, ls[:, d * BRANCH_W:(d + 1) * BRANCH_W], d, d * DIR_W + BRANCH_W))
        if latent:
            oi_ref[0, :, BRANCH_W:2 * BRANCH_W] = (o[0] + o[1]).astype(BF16)

    j = pl.program_id(1)

    @pl.when(j == 0)
    def _():
        body(ctx_ref, False)

    @pl.when(j > 0)
    def _():
        body(x_ref, True)


def _in_projection(x, ctx, mod3, norm_pre, w_in_r, hg_lb, wgk, bgk, *, tm):
    b, seq, d = x.shape
    ctx_len = ctx.shape[1]
    assert ctx_len == tm and tm % CHUNK == 0
    t_all = ctx_len + seq
    n_lat = seq // tm
    n_ch = tm // CHUNK
    nw = w_in_r.shape[1]
    kern = functools.partial(_inproj_kernel, d_model=d, tm=tm)

    def all_rows(w):
        return pl.BlockSpec((1, tm, w), lambda i, j: (i, j, 0))

    def latent_rows(w):
        return pl.BlockSpec((1, tm, w), lambda i, j: (i, jnp.maximum(j - 1, 0), 0))

    return pl.pallas_call(
        kern,
        grid=(b, n_lat + 1),
        in_specs=[
            latent_rows(d),
            pl.BlockSpec((1, tm, d), lambda i, j: (i, 0, 0)),
            pl.BlockSpec((1, 1, N_MOD * d), lambda i, j: (jnp.where(j == 0, 0, i + 1), 0, 0)),
            _const_spec((1, d)),
            _const_spec((d, nw)),
            _const_spec(hg_lb.shape),
            _const_spec(wgk.shape),
            _const_spec(bgk.shape),
        ],
        out_specs=[
            latent_rows(2 * DIR_W),
            all_rows(2 * DIR_W),
            all_rows(DIR_W),
            pl.BlockSpec((1, n_ch, 1, 2 * DIR_W), lambda i, j: (i, j, 0, 0)),
            latent_rows(DIR_W),
        ],
        out_shape=[
            jax.ShapeDtypeStruct((b, seq, 2 * DIR_W), BF16),
            jax.ShapeDtypeStruct((b, t_all, 2 * DIR_W), BF16),
            jax.ShapeDtypeStruct((b, t_all, DIR_W), BF16),
            jax.ShapeDtypeStruct((b, t_all // CHUNK, 1, 2 * DIR_W), F32),
            jax.ShapeDtypeStruct((b, seq, DIR_W), BF16),
        ],
        compiler_params=pltpu.CompilerParams(
            dimension_semantics=("arbitrary", "arbitrary"),
            vmem_limit_bytes=VMEM_LIMIT_BYTES),
        name="in_projection",
    )(x, ctx, mod3, norm_pre, w_in_r, hg_lb, wgk, bgk)


def _scan_kernel(qdf_ref, qdb_ref, kdf_ref, kdb_ref, vf_ref, vb_ref, elf_ref, elb_ref,
                 o_ref, s_ref, *, ctx_steps, n_steps, group):
    s = pl.program_id(1)
    rows_per_step = group * CHUNK

    @pl.when(s == 0)
    def _():
        s_ref[...] = jnp.zeros_like(s_ref)
        o_ref[...] = jnp.zeros_like(o_ref)

    def run(with_out):
        if with_out:
            base = (pl.multiple_of((s - ctx_steps) * rows_per_step, rows_per_step),
                    pl.multiple_of((n_steps - 1 - s) * rows_per_step, rows_per_step))
        for g in range(group):
            for d, (qd_ref, kd_ref, v_ref, el_ref) in enumerate((
                    (qdf_ref, kdf_ref, vf_ref, elf_ref), (qdb_ref, kdb_ref, vb_ref, elb_ref))):
                ci = g if d == 0 else group - 1 - g
                rows = slice(ci * CHUNK, (ci + 1) * CHUNK)
                e_last = el_ref[0, ci]
                outs = []
                for u in range(2 * HEADS):
                    sl = slice(u * HEAD_DIM, (u + 1) * HEAD_DIM)
                    st = s_ref[d * 2 * HEADS + u]
                    if with_out:
                        outs.append(_dot_nt(qd_ref[0, rows, sl], st.astype(BF16)))
                    s_ref[d * 2 * HEADS + u] = (st * e_last[:, sl]
                                                + _dot_tn(v_ref[0, rows, sl], kd_ref[0, rows, sl]))
                if with_out:
                    o_ref[0, pl.ds(base[d] + ci * CHUNK, CHUNK), :] += jnp.concatenate(outs, axis=-1)

    @pl.when(s < ctx_steps)
    def _():
        run(False)

    @pl.when(s >= ctx_steps)
    def _():
        run(True)


def _state_scan(qd, kd, v, el, *, ctx_len, seq, group):
    b = qd.shape[0]
    rows_per_step = group * CHUNK
    assert ctx_len % rows_per_step == 0 and seq % rows_per_step == 0
    ctx_steps = ctx_len // rows_per_step
    n_steps = (ctx_len + seq) // rows_per_step

    def bw_block(s):
        return jnp.where(s < ctx_steps, ctx_steps - 1 - s, n_steps + ctx_steps - 1 - s)

    def fw_rows(i, s):
        return (i, s, 0)

    def bw_rows(i, s):
        return (i, bw_block(s), 1)

    def bw_rows_v(i, s):
        return (i, bw_block(s), 0)

    def fw_lat(i, s):
        return (i, jnp.maximum(s, ctx_steps) - ctx_steps, 0)

    def bw_lat(i, s):
        return (i, n_steps - 1 - jnp.maximum(s, ctx_steps), 1)

    kern = functools.partial(_scan_kernel, ctx_steps=ctx_steps, n_steps=n_steps, group=group)
    blk = (1, rows_per_step, DIR_W)
    eblk = (1, group, 1, DIR_W)
    return pl.pallas_call(
        kern,
        grid=(b, n_steps),
        in_specs=[pl.BlockSpec(blk, fw_lat), pl.BlockSpec(blk, bw_lat),
                  pl.BlockSpec(blk, fw_rows), pl.BlockSpec(blk, bw_rows),
                  pl.BlockSpec(blk, fw_rows), pl.BlockSpec(blk, bw_rows_v),
                  pl.BlockSpec(eblk, lambda i, s: (i, s, 0, 0)),
                  pl.BlockSpec(eblk, lambda i, s: (i, bw_block(s), 0, 1))],
        out_specs=pl.BlockSpec((1, seq, DIR_W), lambda i, s: (i, 0, 0)),
        out_shape=jax.ShapeDtypeStruct((b, seq, DIR_W), F32),
        scratch_shapes=[pltpu.VMEM((4 * HEADS, HEAD_DIM, HEAD_DIM), F32)],
        compiler_params=pltpu.CompilerParams(
            dimension_semantics=("arbitrary", "arbitrary"),
            vmem_limit_bytes=VMEM_LIMIT_BYTES),
        name="state_scan",
    )(qd, qd, kd, kd, v, v, el, el)


def _tail_kernel(oi_ref, os_ref, x_ref, mod_ref, hgn_ref, gln_ref, pre1_ref, post1_ref, pre2_ref, post2_ref,
                 wgh_ref, wgg_ref, wgm_ref, wbh_ref, wbg_ref, wout_ref, wg_ref, wu_ref, wd_ref, out_ref,
                 *, d_model):
    x = x_ref[0]
    mod = mod_ref[0]
    m = [mod[:, i * d_model:(i + 1) * d_model] for i in range(N_MOD)]
    h1 = (_rms(x) * pre1_ref[...] * (1.0 + m[1]) + m[0]).astype(BF16)

    o = oi_ref[0].astype(F32) + os_ref[0]
    gates = (wgh_ref, wgg_ref)
    norms = (hgn_ref, gln_ref)
    ys = []
    for br, wb_ref in enumerate((wbh_ref, wbg_ref)):
        parts = []
        for hd in range(HEADS):
            c0 = br * BRANCH_W + hd * HEAD_DIM
            parts.append(_rms(o[:, c0:c0 + HEAD_DIM]) * norms[br][...])
        on = jnp.concatenate(parts, axis=-1)
        og = (on * _silu(_dot(h1, gates[br][...]))).astype(BF16)
        ys.append(_dot(og, wb_ref[...]))
    gm = _sigmoid(_dot(h1, wgm_ref[...]))
    merged = gm[:, :d_model] * ys[0] + gm[:, d_model:] * ys[1]
    y = _dot(merged.astype(BF16), wout_ref[...])

    z1 = x + _rms(y) * post1_ref[...] * m[2]
    h2 = (_rms(z1) * pre2_ref[...] * (1.0 + m[4]) + m[3]).astype(BF16)
    g = _dot(h2, wg_ref[...])
    u = _dot(h2, wu_ref[...])
    a = (_silu(g) * u).astype(BF16)
    y2 = _dot(a, wd_ref[...])
    out_ref[0] = z1 + _rms(y2) * post2_ref[...] * m[5]


def _tail(oi, o_scan, x, mod3, *consts, tm):
    b, seq, d = x.shape
    kern = functools.partial(_tail_kernel, d_model=d)
    return pl.pallas_call(
        kern,
        grid=(b, seq // tm),
        in_specs=[
            pl.BlockSpec((1, tm, DIR_W), lambda i, j: (i, j, 0)),
            pl.BlockSpec((1, tm, DIR_W), lambda i, j: (i, j, 0)),
            pl.BlockSpec((1, tm, d), lambda i, j: (i, j, 0)),
            pl.BlockSpec((1, 1, N_MOD * d), lambda i, j: (i + 1, 0, 0)),
        ] + [_const_spec(c.shape) for c in consts],
        out_specs=pl.BlockSpec((1, tm, d), lambda i, j: (i, j, 0)),
        out_shape=jax.ShapeDtypeStruct((b, seq, d), F32),
        compiler_params=pltpu.CompilerParams(
            dimension_semantics=("arbitrary", "arbitrary"),
            vmem_limit_bytes=VMEM_LIMIT_BYTES),
        name="mixer_out_ffn",
    )(oi, o_scan, x, mod3, *consts)


def kernel(x, c, ctx, c_ctx, w_mod, b_mod, norm_pre1, norm_post1, norm_pre2, norm_post2, w_in, hg_lb,
           hg_onorm, gla_w_gk, gla_b_gk, gla_onorm, w_br_hg, w_br_gla, w_out, w_ff_gate, w_ff_up, w_ff_down):
    b, seq, d = x.shape
    ctx_len = ctx.shape[1]
    assert w_mod.shape[0] == 1 and d == DIR_W
    assert seq % CHUNK == 0 and ctx_len % CHUNK == 0
    tm = ctx_len

    c_all = jnp.zeros((MOD_ROWS, d), F32).at[0].set(c_ctx).at[1:b + 1].set(c)
    mod3 = _modulation(c_all, w_mod.reshape(d, N_MOD * d), b_mod).reshape(MOD_ROWS, 1, N_MOD * d)

    def bf(a):
        return a.reshape(a.shape[1:]).astype(BF16)

    w = bf(w_in)
    lr_w = 2 * GLA_RANK
    wgk = jnp.zeros((LR_PAD, 2 * BRANCH_W), F32)
    wgk = wgk.at[:GLA_RANK, :BRANCH_W].set(gla_w_gk[0, 0]).at[GLA_RANK:lr_w, BRANCH_W:].set(gla_w_gk[0, 1])
    bgk = gla_b_gk.reshape(1, 2 * BRANCH_W)

    qd, kd, v, el, oi = _in_projection(
        x, ctx, mod3, norm_pre1, w, hg_lb, wgk.astype(BF16), bgk, tm=tm)
    o_scan = _state_scan(qd, kd, v, el, ctx_len=ctx_len, seq=seq, group=tm // CHUNK)
    return _tail(oi, o_scan, x, mod3, hg_onorm, gla_onorm, norm_pre1, norm_post1, norm_pre2, norm_post2,
                 w[:, G_HGATE * BRANCH_W:(G_HGATE + 1) * BRANCH_W],
                 w[:, G_GGATE * BRANCH_W:(G_GGATE + 1) * BRANCH_W],
                 w[:, LR_COL + lr_w:],
                 bf(w_br_hg), bf(w_br_gla), bf(w_out), bf(w_ff_gate), bf(w_ff_up), bf(w_ff_down),
                 tm=TAIL_ROWS)
```

```python
import functools

import jax
import jax.numpy as jnp
from jax import lax
from jax.experimental import pallas as pl
from jax.experimental.pallas import tpu as pltpu

F32 = jnp.float32
BF16 = jnp.bfloat16

EPS = 1e-6
LOG2E = 1.4426950408889634
CHUNK = 64
N_MOD = 6
HEADS = 4
HEAD_DIM = 128
BRANCH_W = HEADS * HEAD_DIM
DIR_W = 2 * BRANCH_W
GLA_RANK = 16
GLA_GATE_NORM = 16.0
LR_PAD = 128
G_HQ, G_HI, G_HF, G_HB, G_HGATE, G_GQ, G_GK, G_GV, G_GGATE = range(9)
LR_COL = 9 * BRANCH_W
MOD_ROWS = 16
VMEM_LIMIT_BYTES = 56 * 1024 * 1024
TAIL_ROWS = 512


def _dot(a, b):
    return jnp.dot(a, b, preferred_element_type=F32)


def _dot_nt(a, b):
    return lax.dot_general(a, b, (((1,), (1,)), ((), ())), preferred_element_type=F32)


def _dot_tn(a, b):
    return lax.dot_general(a, b, (((0,), (0,)), ((), ())), preferred_element_type=F32)


def _sigmoid(x):
    return 1.0 / (1.0 + jnp.exp(-x))


def _silu(x):
    return x * _sigmoid(x)


def _rms(x):
    return x * lax.rsqrt(jnp.mean(x * x, axis=-1, keepdims=True) + EPS)


def _const_spec(shape):
    nd = len(shape)
    return pl.BlockSpec(shape, lambda *_: (0,) * nd, pipeline_mode=pl.Buffered(1))


def _mod_kernel(c_ref, w_ref, b_ref, o_ref):
    a = _silu(c_ref[...]).astype(BF16)
    o_ref[...] = _dot(a, w_ref[...].astype(BF16)) + b_ref[...]


def _modulation(c_all, w_mod, b_mod):
    d = c_all.shape[1]
    n = w_mod.shape[1]
    blk = 1024
    return pl.pallas_call(
        _mod_kernel,
        grid=(n // blk,),
        in_specs=[pl.BlockSpec((MOD_ROWS, d), lambda j: (0, 0)),
                  pl.BlockSpec((d, blk), lambda j: (0, j)),
                  pl.BlockSpec((1, blk), lambda j: (0, j))],
        out_specs=pl.BlockSpec((MOD_ROWS, blk), lambda j: (0, j)),
        out_shape=jax.ShapeDtypeStruct((MOD_ROWS, n), F32),
        name="modulation",
    )(c_all, w_mod, b_mod)


def _inproj_kernel(x_ref, ctx_ref, mod_ref, pre_ref, w_ref, lb_ref, wgk_ref, bgk_ref,
                   qd_ref, kd_ref, v_ref, el_ref, oi_ref, *, d_model, tm):
    n_ch = tm // CHUNK
    mids = (CHUNK // 2, CHUNK // 2 - 1)
    lasts = (CHUNK - 1, 0)

    def body(z_ref, latent):
        mod = mod_ref[0]
        shift, scale = mod[:, :d_model], mod[:, d_model:2 * d_model]
        h = (_rms(z_ref[0]) * pre_ref[...] * (1.0 + scale) + shift).astype(BF16)

        lbr = lb_ref[...]
        e = jnp.exp(lbr - jnp.max(lbr, axis=0, keepdims=True))
        lb = e[0] / jnp.sum(e, axis=0)

        row = lax.broadcasted_iota(jnp.int32, (tm, tm), 0)
        col = lax.broadcasted_iota(jnp.int32, (tm, tm), 1)
        shift_bits = CHUNK.bit_length() - 1
        same = jnp.right_shift(row, shift_bits) == jnp.right_shift(col, shift_bits)
        masks = (same & (col <= row), same & (col >= row))
        tris = tuple(jnp.where(m, 1.0, 0.0).astype(BF16) for m in masks)

        def proj(g):
            return _dot(h, w_ref[:, g * BRANCH_W:(g + 1) * BRANCH_W])

        def prep(q, k, lf, d, col0):
            lf2 = lf * LOG2E
            hi = lf2.astype(BF16)
            md = (lf2 - hi.astype(F32)).astype(BF16)
            c = _dot(tris[d], hi) + _dot(tris[d], md)
            qts, kts = [], []
            for n in range(n_ch):
                rows = slice(n * CHUNK, (n + 1) * CHUNK)
                cj = c[rows]
                cl = cj[lasts[d]:lasts[d] + 1]
                el_ref[0, n, :, col0:col0 + BRANCH_W] = jnp.exp2(cl)
                if not latent:
                    kd_ref[0, rows, col0:col0 + BRANCH_W] = (k[rows] * jnp.exp2(cl - cj)).astype(BF16)
                    continue
                r = cj[mids[d]:mids[d] + 1]
                dq = cj - r
                qt = q[rows] * jnp.exp2(dq)
                kt = k[rows] * jnp.exp2(-dq)
                qd_ref[0, rows, col0:col0 + BRANCH_W] = (qt * jnp.exp2(r)).astype(BF16)
                kd_ref[0, rows, col0:col0 + BRANCH_W] = (kt * jnp.exp2(cl - r)).astype(BF16)
                qts.append(qt.astype(BF16))
                kts.append(kt.astype(BF16))
            if not latent:
                return None
            return jnp.concatenate(qts, axis=0), jnp.concatenate(kts, axis=0)

        def intra(qk, v, d):
            if not latent:
                return None
            qt, kt = qk
            outs = []
            for hd in range(HEADS):
                sl = slice(hd * HEAD_DIM, (hd + 1) * HEAD_DIM)
                a = _dot_nt(qt[:, sl], kt[:, sl])
                a = jnp.where(masks[d], a, 0.0).astype(BF16)
                outs.append(_dot(a, v[:, sl]))
            return jnp.concatenate(outs, axis=-1)

        def hg_gate(raw, d):
            lbd = lb[d:d + 1]
            f = lbd + (1.0 - lbd) * _sigmoid(raw)
            return 1.0 - f, jnp.log(f)

        pq_h = proj(G_HQ) if latent else None
        pv_h = proj(G_HI)
        pf = [proj(G_HF + d) for d in range(2)]
        q_h = _silu(pq_h) if latent else None
        v_h = pv_h.astype(BF16)
        v_ref[0, :, 0:BRANCH_W] = v_h
        qk_h0 = prep(q_h, *hg_gate(pf[0], 0), 0, 0)
        pq_g = proj(G_GQ) if latent else None
        pk_g = proj(G_GK)
        pv_g = proj(G_GV)
        lr = _dot(h, w_ref[:, LR_COL:LR_COL + LR_PAD]).astype(BF16)
        xg = _dot(lr, wgk_ref[...]) + bgk_ref[...]
        qk_h1 = prep(q_h, *hg_gate(pf[1], 1), 1, DIR_W)
        o0 = intra(qk_h0, v_h, 0)
        o1 = intra(qk_h1, v_h, 1)
        q_g = pq_g * HEAD_DIM ** -0.5 if latent else None
        v_g = pv_g.astype(BF16)
        v_ref[0, :, BRANCH_W:2 * BRANCH_W] = v_g
        ls = (jnp.minimum(xg, 0.0) - jnp.log(1.0 + jnp.exp(-jnp.abs(xg)))) * (1.0 / GLA_GATE_NORM)
        qk_g0 = prep(q_g, pk_g, ls[:, :BRANCH_W], 0, BRANCH_W)
        if latent:
            oi_ref[0, :, 0:BRANCH_W] = (o0 + o1).astype(BF16)
        qk_g1 = prep(q_g, pk_g, ls[:, BRANCH_W:], 1, DIR_W + BRANCH_W)
        o0 = intra(qk_g0, v_g, 0)
        o1 = intra(qk_g1, v_g, 1)
        if latent:
            oi_ref[0, :, BRANCH_W:2 * BRANCH_W] = (o0 + o1).astype(BF16)

    j = pl.program_id(1)

    @pl.when(j == 0)
    def _():
        body(ctx_ref, False)

    @pl.when(j > 0)
    def _():
        body(x_ref, True)


def _in_projection(x, ctx, mod3, norm_pre, w_in_r, hg_lb, wgk, bgk, *, tm):
    b, seq, d = x.shape
    ctx_len = ctx.shape[1]
    assert ctx_len == tm and tm % CHUNK == 0
    t_all = ctx_len + seq
    n_lat = seq // tm
    n_ch = tm // CHUNK
    nw = w_in_r.shape[1]
    kern = functools.partial(_inproj_kernel, d_model=d, tm=tm)

    def all_rows(w):
        return pl.BlockSpec((1, tm, w), lambda i, j: (i, j, 0))

    def latent_rows(w):
        return pl.BlockSpec((1, tm, w), lambda i, j: (i, jnp.maximum(j - 1, 0), 0))

    return pl.pallas_call(
        kern,
        grid=(b, n_lat + 1),
        in_specs=[
            latent_rows(d),
            pl.BlockSpec((1, tm, d), lambda i, j: (i, 0, 0)),
            pl.BlockSpec((1, 1, N_MOD * d), lambda i, j: (jnp.where(j == 0, 0, i + 1), 0, 0)),
            _const_spec((1, d)),
            _const_spec((d, nw)),
            _const_spec(hg_lb.shape),
            _const_spec(wgk.shape),
            _const_spec(bgk.shape),
        ],
        out_specs=[
            latent_rows(2 * DIR_W),
            all_rows(2 * DIR_W),
            all_rows(DIR_W),
            pl.BlockSpec((1, n_ch, 1, 2 * DIR_W), lambda i, j: (i, j, 0, 0)),
            latent_rows(DIR_W),
        ],
        out_shape=[
            jax.ShapeDtypeStruct((b, seq, 2 * DIR_W), BF16),
            jax.ShapeDtypeStruct((b, t_all, 2 * DIR_W), BF16),
            jax.ShapeDtypeStruct((b, t_all, DIR_W), BF16),
            jax.ShapeDtypeStruct((b, t_all // CHUNK, 1, 2 * DIR_W), F32),
            jax.ShapeDtypeStruct((b, seq, DIR_W), BF16),
        ],
        compiler_params=pltpu.CompilerParams(
            dimension_semantics=("arbitrary", "arbitrary"),
            vmem_limit_bytes=VMEM_LIMIT_BYTES),
        name="in_projection",
    )(x, ctx, mod3, norm_pre, w_in_r, hg_lb, wgk, bgk)


def _scan_kernel(qdf_ref, qdb_ref, kdf_ref, kdb_ref, vf_ref, vb_ref, elf_ref, elb_ref,
                 o_ref, s_ref, *, ctx_steps, n_steps, group):
    s = pl.program_id(1)
    rows_per_step = group * CHUNK

    @pl.when(s == 0)
    def _():
        s_ref[...] = jnp.zeros_like(s_ref)
        o_ref[...] = jnp.zeros_like(o_ref)

    def run(with_out):
        if with_out:
            base = (pl.multiple_of((s - ctx_steps) * rows_per_step, rows_per_step),
                    pl.multiple_of((n_steps - 1 - s) * rows_per_step, rows_per_step))
        for g in range(group):
            for d, (qd_ref, kd_ref, v_ref, el_ref) in enumerate((
                    (qdf_ref, kdf_ref, vf_ref, elf_ref), (qdb_ref, kdb_ref, vb_ref, elb_ref))):
                ci = g if d == 0 else group - 1 - g
                rows = slice(ci * CHUNK, (ci + 1) * CHUNK)
                e_last = el_ref[0, ci]
                outs = []
                for u in range(2 * HEADS):
                    sl = slice(u * HEAD_DIM, (u + 1) * HEAD_DIM)
                    st = s_ref[d * 2 * HEADS + u]
                    if with_out:
                        outs.append(_dot_nt(qd_ref[0, rows, sl], st.astype(BF16)))
                    s_ref[d * 2 * HEADS + u] = (st * e_last[:, sl]
                                                + _dot_tn(v_ref[0, rows, sl], kd_ref[0, rows, sl]))
                if with_out:
                    o_ref[0, pl.ds(base[d] + ci * CHUNK, CHUNK), :] += jnp.concatenate(outs, axis=-1)

    @pl.when(s < ctx_steps)
    def _():
        run(False)

    @pl.when(s >= ctx_steps)
    def _():
        run(True)


def _state_scan(qd, kd, v, el, *, ctx_len, seq, group):
    b = qd.shape[0]
    rows_per_step = group * CHUNK
    assert ctx_len % rows_per_step == 0 and seq % rows_per_step == 0
    ctx_steps = ctx_len // rows_per_step
    n_steps = (ctx_len + seq) // rows_per_step

    def bw_block(s):
        return jnp.where(s < ctx_steps, ctx_steps - 1 - s, n_steps + ctx_steps - 1 - s)

    def fw_rows(i, s):
        return (i, s, 0)

    def bw_rows(i, s):
        return (i, bw_block(s), 1)

    def bw_rows_v(i, s):
        return (i, bw_block(s), 0)

    def fw_lat(i, s):
        return (i, jnp.maximum(s, ctx_steps) - ctx_steps, 0)

    def bw_lat(i, s):
        return (i, n_steps - 1 - jnp.maximum(s, ctx_steps), 1)

    kern = functools.partial(_scan_kernel, ctx_steps=ctx_steps, n_steps=n_steps, group=group)
    blk = (1, rows_per_step, DIR_W)
    eblk = (1, group, 1, DIR_W)
    return pl.pallas_call(
        kern,
        grid=(b, n_steps),
        in_specs=[pl.BlockSpec(blk, fw_lat), pl.BlockSpec(blk, bw_lat),
                  pl.BlockSpec(blk, fw_rows), pl.BlockSpec(blk, bw_rows),
                  pl.BlockSpec(blk, fw_rows), pl.BlockSpec(blk, bw_rows_v),
                  pl.BlockSpec(eblk, lambda i, s: (i, s, 0, 0)),
                  pl.BlockSpec(eblk, lambda i, s: (i, bw_block(s), 0, 1))],
        out_specs=pl.BlockSpec((1, seq, DIR_W), lambda i, s: (i, 0, 0)),
        out_shape=jax.ShapeDtypeStruct((b, seq, DIR_W), F32),
        scratch_shapes=[pltpu.VMEM((4 * HEADS, HEAD_DIM, HEAD_DIM), F32)],
        compiler_params=pltpu.CompilerParams(
            dimension_semantics=("arbitrary", "arbitrary"),
            vmem_limit_bytes=VMEM_LIMIT_BYTES),
        name="state_scan",
    )(qd, qd, kd, kd, v, v, el, el)


def _tail_kernel(oi_ref, os_ref, x_ref, mod_ref, hgn_ref, gln_ref, pre1_ref, post1_ref, pre2_ref, post2_ref,
                 wgh_ref, wgg_ref, wgm_ref, wbh_ref, wbg_ref, wout_ref, wg_ref, wu_ref, wd_ref, out_ref,
                 *, d_model):
    x = x_ref[0]
    mod = mod_ref[0]
    m = [mod[:, i * d_model:(i + 1) * d_model] for i in range(N_MOD)]
    h1 = (_rms(x) * pre1_ref[...] * (1.0 + m[1]) + m[0]).astype(BF16)

    o = oi_ref[0].astype(F32) + os_ref[0]
    gates = (wgh_ref, wgg_ref)
    norms = (hgn_ref, gln_ref)
    ys = []
    for br, wb_ref in enumerate((wbh_ref, wbg_ref)):
        parts = []
        for hd in range(HEADS):
            c0 = br * BRANCH_W + hd * HEAD_DIM
            parts.append(_rms(o[:, c0:c0 + HEAD_DIM]) * norms[br][...])
        on = jnp.concatenate(parts, axis=-1)
        og = (on * _silu(_dot(h1, gates[br][...]))).astype(BF16)
        ys.append(_dot(og, wb_ref[...]))
    gm = _sigmoid(_dot(h1, wgm_ref[...]))
    merged = gm[:, :d_model] * ys[0] + gm[:, d_model:] * ys[1]
    y = _dot(merged.astype(BF16), wout_ref[...])

    z1 = x + _rms(y) * post1_ref[...] * m[2]
    h2 = (_rms(z1) * pre2_ref[...] * (1.0 + m[4]) + m[3]).astype(BF16)
    g = _dot(h2, wg_ref[...])
    u = _dot(h2, wu_ref[...])
    a = (_silu(g) * u).astype(BF16)
    y2 = _dot(a, wd_ref[...])
    out_ref[0] = z1 + _rms(y2) * post2_ref[...] * m[5]


def _tail(oi, o_scan, x, mod3, *consts, tm):
    b, seq, d = x.shape
    kern = functools.partial(_tail_kernel, d_model=d)
    return pl.pallas_call(
        kern,
        grid=(b, seq // tm),
        in_specs=[
            pl.BlockSpec((1, tm, DIR_W), lambda i, j: (i, j, 0)),
            pl.BlockSpec((1, tm, DIR_W), lambda i, j: (i, j, 0)),
            pl.BlockSpec((1, tm, d), lambda i, j: (i, j, 0)),
            pl.BlockSpec((1, 1, N_MOD * d), lambda i, j: (i + 1, 0, 0)),
        ] + [_const_spec(c.shape) for c in consts],
        out_specs=pl.BlockSpec((1, tm, d), lambda i, j: (i, j, 0)),
        out_shape=jax.ShapeDtypeStruct((b, seq, d), F32),
        compiler_params=pltpu.CompilerParams(
            dimension_semantics=("arbitrary", "arbitrary"),
            vmem_limit_bytes=VMEM_LIMIT_BYTES),
        name="mixer_out_ffn",
    )(oi, o_scan, x, mod3, *consts)


def kernel(x, c, ctx, c_ctx, w_mod, b_mod, norm_pre1, norm_post1, norm_pre2, norm_post2, w_in, hg_lb,
           hg_onorm, gla_w_gk, gla_b_gk, gla_onorm, w_br_hg, w_br_gla, w_out, w_ff_gate, w_ff_up, w_ff_down):
    b, seq, d = x.shape
    ctx_len = ctx.shape[1]
    assert w_mod.shape[0] == 1 and d == DIR_W
    assert seq % CHUNK == 0 and ctx_len % CHUNK == 0
    tm = ctx_len

    c_all = jnp.zeros((MOD_ROWS, d), F32).at[0].set(c_ctx).at[1:b + 1].set(c)
    mod3 = _modulation(c_all, w_mod.reshape(d, N_MOD * d), b_mod).reshape(MOD_ROWS, 1, N_MOD * d)

    def bf(a):
        return a.reshape(a.shape[1:]).astype(BF16)

    w = bf(w_in)
    lr_w = 2 * GLA_RANK
    wgk = jnp.zeros((LR_PAD, 2 * BRANCH_W), F32)
    wgk = wgk.at[:GLA_RANK, :BRANCH_W].set(gla_w_gk[0, 0]).at[GLA_RANK:lr_w, BRANCH_W:].set(gla_w_gk[0, 1])
    bgk = gla_b_gk.reshape(1, 2 * BRANCH_W)

    qd, kd, v, el, oi = _in_projection(
        x, ctx, mod3, norm_pre1, w, hg_lb, wgk.astype(BF16), bgk, tm=tm)
    o_scan = _state_scan(qd, kd, v, el, ctx_len=ctx_len, seq=seq, group=tm // CHUNK)
    return _tail(oi, o_scan, x, mod3, hg_onorm, gla_onorm, norm_pre1, norm_post1, norm_pre2, norm_post2,
                 w[:, G_HGATE * BRANCH_W:(G_HGATE + 1) * BRANCH_W],
                 w[:, G_GGATE * BRANCH_W:(G_GGATE + 1) * BRANCH_W],
                 w[:, LR_COL + lr_w:],
                 bf(w_br_hg), bf(w_br_gla), bf(w_out), bf(w_ff_gate), bf(w_ff_up), bf(w_ff_down),
                 tm=TAIL_ROWS)
```

```python
import functools

import jax
import jax.numpy as jnp
from jax import lax
from jax.experimental import pallas as pl
from jax.experimental.pallas import tpu as pltpu

F32 = jnp.float32
BF16 = jnp.bfloat16

EPS = 1e-6
LOG2E = 1.4426950408889634
CHUNK = 64
N_MOD = 6
HEADS = 4
HEAD_DIM = 128
BRANCH_W = HEADS * HEAD_DIM
DIR_W = 2 * BRANCH_W
GLA_RANK = 16
GLA_GATE_NORM = 16.0
LR_PAD = 128
G_HQ, G_HI, G_HF, G_HB, G_HGATE, G_GQ, G_GK, G_GV, G_GGATE = range(9)
LR_COL = 9 * BRANCH_W
MOD_ROWS = 16
VMEM_LIMIT_BYTES = 56 * 1024 * 1024
TAIL_ROWS = 512
TAIL_STREAMS = 2


def _dot(a, b):
    return jnp.dot(a, b, preferred_element_type=F32)


def _dot_nt(a, b):
    return lax.dot_general(a, b, (((1,), (1,)), ((), ())), preferred_element_type=F32)


def _dot_tn(a, b):
    return lax.dot_general(a, b, (((0,), (0,)), ((), ())), preferred_element_type=F32)


def _sigmoid(x):
    return 1.0 / (1.0 + jnp.exp(-x))


def _silu(x):
    return x * _sigmoid(x)


def _rms(x):
    return x * lax.rsqrt(jnp.mean(x * x, axis=-1, keepdims=True) + EPS)


def _const_spec(shape):
    nd = len(shape)
    return pl.BlockSpec(shape, lambda *_: (0,) * nd, pipeline_mode=pl.Buffered(1))


def _mod_kernel(c_ref, w_ref, b_ref, o_ref):
    a = _silu(c_ref[...]).astype(BF16)
    o_ref[...] = _dot(a, w_ref[...].astype(BF16)) + b_ref[...]


def _modulation(c_all, w_mod, b_mod):
    d = c_all.shape[1]
    n = w_mod.shape[1]
    blk = 1024
    return pl.pallas_call(
        _mod_kernel,
        grid=(n // blk,),
        in_specs=[pl.BlockSpec((MOD_ROWS, d), lambda j: (0, 0)),
                  pl.BlockSpec((d, blk), lambda j: (0, j)),
                  pl.BlockSpec((1, blk), lambda j: (0, j))],
        out_specs=pl.BlockSpec((MOD_ROWS, blk), lambda j: (0, j)),
        out_shape=jax.ShapeDtypeStruct((MOD_ROWS, n), F32),
        name="modulation",
    )(c_all, w_mod, b_mod)


def _inproj_kernel(x_ref, ctx_ref, mod_ref, pre_ref, w_ref, lb_ref, wgk_ref, bgk_ref,
                   qd_ref, kd_ref, v_ref, el_ref, oi_ref, *, d_model, tm):
    n_ch = tm // CHUNK
    mids = (CHUNK // 2, CHUNK // 2 - 1)
    lasts = (CHUNK - 1, 0)

    def body(z_ref, latent):
        mod = mod_ref[0]
        shift, scale = mod[:, :d_model], mod[:, d_model:2 * d_model]
        h = (_rms(z_ref[0]) * pre_ref[...] * (1.0 + scale) + shift).astype(BF16)

        lbr = lb_ref[...]
        e = jnp.exp(lbr - jnp.max(lbr, axis=0, keepdims=True))
        lb = e[0] / jnp.sum(e, axis=0)

        row = lax.broadcasted_iota(jnp.int32, (tm, tm), 0)
        col = lax.broadcasted_iota(jnp.int32, (tm, tm), 1)
        shift_bits = CHUNK.bit_length() - 1
        same = jnp.right_shift(row, shift_bits) == jnp.right_shift(col, shift_bits)
        masks = (same & (col <= row), same & (col >= row))
        tris = tuple(jnp.where(m, 1.0, 0.0).astype(BF16) for m in masks)

        def proj(g):
            return _dot(h, w_ref[:, g * BRANCH_W:(g + 1) * BRANCH_W])

        def prep(q, k, lf, d, col0):
            lf2 = lf * LOG2E
            hi = lf2.astype(BF16)
            md = (lf2 - hi.astype(F32)).astype(BF16)
            c = _dot(tris[d], hi) + _dot(tris[d], md)
            qts, kts = [], []
            for n in range(n_ch):
                rows = slice(n * CHUNK, (n + 1) * CHUNK)
                cj = c[rows]
                cl = cj[lasts[d]:lasts[d] + 1]
                el_ref[0, n, :, col0:col0 + BRANCH_W] = jnp.exp2(cl)
                if not latent:
                    kd_ref[0, rows, col0:col0 + BRANCH_W] = (k[rows] * jnp.exp2(cl - cj)).astype(BF16)
                    continue
                r = cj[mids[d]:mids[d] + 1]
                dq = cj - r
                qt = q[rows] * jnp.exp2(dq)
                kt = k[rows] * jnp.exp2(-dq)
                qd_ref[0, rows, col0:col0 + BRANCH_W] = (qt * jnp.exp2(r)).astype(BF16)
                kd_ref[0, rows, col0:col0 + BRANCH_W] = (kt * jnp.exp2(cl - r)).astype(BF16)
                qts.append(qt.astype(BF16))
                kts.append(kt.astype(BF16))
            if not latent:
                return None
            return jnp.concatenate(qts, axis=0), jnp.concatenate(kts, axis=0)

        def intra(qk, v, d):
            if not latent:
                return None
            qt, kt = qk
            outs = []
            for hd in range(HEADS):
                sl = slice(hd * HEAD_DIM, (hd + 1) * HEAD_DIM)
                a = _dot_nt(qt[:, sl], kt[:, sl])
                a = jnp.where(masks[d], a, 0.0).astype(BF16)
                outs.append(_dot(a, v[:, sl]))
            return jnp.concatenate(outs, axis=-1)

        def hg_gate(raw, d):
            lbd = lb[d:d + 1]
            f = lbd + (1.0 - lbd) * _sigmoid(raw)
            return 1.0 - f, jnp.log(f)

        pq_h = proj(G_HQ) if latent else None
        pv_h = proj(G_HI)
        pf = [proj(G_HF + d) for d in range(2)]
        q_h = _silu(pq_h) if latent else None
        v_h = pv_h.astype(BF16)
        v_ref[0, :, 0:BRANCH_W] = v_h
        qk_h0 = prep(q_h, *hg_gate(pf[0], 0), 0, 0)
        pq_g = proj(G_GQ) if latent else None
        pk_g = proj(G_GK)
        pv_g = proj(G_GV)
        lr = _dot(h, w_ref[:, LR_COL:LR_COL + LR_PAD]).astype(BF16)
        xg = _dot(lr, wgk_ref[...]) + bgk_ref[...]
        qk_h1 = prep(q_h, *hg_gate(pf[1], 1), 1, DIR_W)
        o0 = intra(qk_h0, v_h, 0)
        o1 = intra(qk_h1, v_h, 1)
        q_g = pq_g * HEAD_DIM ** -0.5 if latent else None
        v_g = pv_g.astype(BF16)
        v_ref[0, :, BRANCH_W:2 * BRANCH_W] = v_g
        ls = (jnp.minimum(xg, 0.0) - jnp.log(1.0 + jnp.exp(-jnp.abs(xg)))) * (1.0 / GLA_GATE_NORM)
        qk_g0 = prep(q_g, pk_g, ls[:, :BRANCH_W], 0, BRANCH_W)
        if latent:
            oi_ref[0, :, 0:BRANCH_W] = (o0 + o1).astype(BF16)
        qk_g1 = prep(q_g, pk_g, ls[:, BRANCH_W:], 1, DIR_W + BRANCH_W)
        o0 = intra(qk_g0, v_g, 0)
        o1 = intra(qk_g1, v_g, 1)
        if latent:
            oi_ref[0, :, BRANCH_W:2 * BRANCH_W] = (o0 + o1).astype(BF16)

    j = pl.program_id(1)

    @pl.when(j == 0)
    def _():
        body(ctx_ref, False)

    @pl.when(j > 0)
    def _():
        body(x_ref, True)


def _in_projection(x, ctx, mod3, norm_pre, w_in_r, hg_lb, wgk, bgk, *, tm):
    b, seq, d = x.shape
    ctx_len = ctx.shape[1]
    assert ctx_len == tm and tm % CHUNK == 0
    t_all = ctx_len + seq
    n_lat = seq // tm
    n_ch = tm // CHUNK
    nw = w_in_r.shape[1]
    kern = functools.partial(_inproj_kernel, d_model=d, tm=tm)

    def all_rows(w):
        return pl.BlockSpec((1, tm, w), lambda i, j: (i, j, 0))

    def latent_rows(w):
        return pl.BlockSpec((1, tm, w), lambda i, j: (i, jnp.maximum(j - 1, 0), 0))

    return pl.pallas_call(
        kern,
        grid=(b, n_lat + 1),
        in_specs=[
            latent_rows(d),
            pl.BlockSpec((1, tm, d), lambda i, j: (i, 0, 0)),
            pl.BlockSpec((1, 1, N_MOD * d), lambda i, j: (jnp.where(j == 0, 0, i + 1), 0, 0)),
            _const_spec((1, d)),
            _const_spec((d, nw)),
            _const_spec(hg_lb.shape),
            _const_spec(wgk.shape),
            _const_spec(bgk.shape),
        ],
        out_specs=[
            latent_rows(2 * DIR_W),
            all_rows(2 * DIR_W),
            all_rows(DIR_W),
            pl.BlockSpec((1, n_ch, 1, 2 * DIR_W), lambda i, j: (i, j, 0, 0)),
            latent_rows(DIR_W),
        ],
        out_shape=[
            jax.ShapeDtypeStruct((b, seq, 2 * DIR_W), BF16),
            jax.ShapeDtypeStruct((b, t_all, 2 * DIR_W), BF16),
            jax.ShapeDtypeStruct((b, t_all, DIR_W), BF16),
            jax.ShapeDtypeStruct((b, t_all // CHUNK, 1, 2 * DIR_W), F32),
            jax.ShapeDtypeStruct((b, seq, DIR_W), BF16),
        ],
        compiler_params=pltpu.CompilerParams(
            dimension_semantics=("arbitrary", "arbitrary"),
            vmem_limit_bytes=VMEM_LIMIT_BYTES),
        name="in_projection",
    )(x, ctx, mod3, norm_pre, w_in_r, hg_lb, wgk, bgk)


def _scan_kernel(qdf_ref, qdb_ref, kdf_ref, kdb_ref, vf_ref, vb_ref, elf_ref, elb_ref,
                 o_ref, s_ref, *, ctx_steps, n_steps, group):
    s = pl.program_id(1)
    rows_per_step = group * CHUNK

    @pl.when(s == 0)
    def _():
        s_ref[...] = jnp.zeros_like(s_ref)
        o_ref[...] = jnp.zeros_like(o_ref)

    def run(with_out):
        if with_out:
            base = (pl.multiple_of((s - ctx_steps) * rows_per_step, rows_per_step),
                    pl.multiple_of((n_steps - 1 - s) * rows_per_step, rows_per_step))
        for g in range(group):
            for d, (qd_ref, kd_ref, v_ref, el_ref) in enumerate((
                    (qdf_ref, kdf_ref, vf_ref, elf_ref), (qdb_ref, kdb_ref, vb_ref, elb_ref))):
                ci = g if d == 0 else group - 1 - g
                rows = slice(ci * CHUNK, (ci + 1) * CHUNK)
                e_last = el_ref[0, ci]
                outs = []
                for u in range(2 * HEADS):
                    sl = slice(u * HEAD_DIM, (u + 1) * HEAD_DIM)
                    st = s_ref[d * 2 * HEADS + u]
                    if with_out:
                        outs.append(_dot_nt(qd_ref[0, rows, sl], st.astype(BF16)))
                    s_ref[d * 2 * HEADS + u] = (st * e_last[:, sl]
                                                + _dot_tn(v_ref[0, rows, sl], kd_ref[0, rows, sl]))
                if with_out:
                    o_ref[0, pl.ds(base[d] + ci * CHUNK, CHUNK), :] += jnp.concatenate(outs, axis=-1)

    @pl.when(s < ctx_steps)
    def _():
        run(False)

    @pl.when(s >= ctx_steps)
    def _():
        run(True)


def _state_scan(qd, kd, v, el, *, ctx_len, seq, group):
    b = qd.shape[0]
    rows_per_step = group * CHUNK
    assert ctx_len % rows_per_step == 0 and seq % rows_per_step == 0
    ctx_steps = ctx_len // rows_per_step
    n_steps = (ctx_len + seq) // rows_per_step

    def bw_block(s):
        return jnp.where(s < ctx_steps, ctx_steps - 1 - s, n_steps + ctx_steps - 1 - s)

    def fw_rows(i, s):
        return (i, s, 0)

    def bw_rows(i, s):
        return (i, bw_block(s), 1)

    def bw_rows_v(i, s):
        return (i, bw_block(s), 0)

    def fw_lat(i, s):
        return (i, jnp.maximum(s, ctx_steps) - ctx_steps, 0)

    def bw_lat(i, s):
        return (i, n_steps - 1 - jnp.maximum(s, ctx_steps), 1)

    kern = functools.partial(_scan_kernel, ctx_steps=ctx_steps, n_steps=n_steps, group=group)
    blk = (1, rows_per_step, DIR_W)
    eblk = (1, group, 1, DIR_W)
    return pl.pallas_call(
        kern,
        grid=(b, n_steps),
        in_specs=[pl.BlockSpec(blk, fw_lat), pl.BlockSpec(blk, bw_lat),
                  pl.BlockSpec(blk, fw_rows), pl.BlockSpec(blk, bw_rows),
                  pl.BlockSpec(blk, fw_rows), pl.BlockSpec(blk, bw_rows_v),
                  pl.BlockSpec(eblk, lambda i, s: (i, s, 0, 0)),
                  pl.BlockSpec(eblk, lambda i, s: (i, bw_block(s), 0, 1))],
        out_specs=pl.BlockSpec((1, seq, DIR_W), lambda i, s: (i, 0, 0)),
        out_shape=jax.ShapeDtypeStruct((b, seq, DIR_W), F32),
        scratch_shapes=[pltpu.VMEM((4 * HEADS, HEAD_DIM, HEAD_DIM), F32)],
        compiler_params=pltpu.CompilerParams(
            dimension_semantics=("arbitrary", "arbitrary"),
            vmem_limit_bytes=VMEM_LIMIT_BYTES),
        name="state_scan",
    )(qd, qd, kd, kd, v, v, el, el)


def _tail_kernel(oi_ref, os_ref, x_ref, mod_ref, hgn_ref, gln_ref, pre1_ref, post1_ref, pre2_ref, post2_ref,
                 wgh_ref, wgg_ref, wgm_ref, wbh_ref, wbg_ref, wout_ref, wg_ref, wu_ref, wd_ref, out_ref,
                 *, d_model, n_streams):
    mod = mod_ref[0]
    m = [mod[:, i * d_model:(i + 1) * d_model] for i in range(N_MOD)]
    rows_per_stream = x_ref.shape[1] // n_streams
    gates = (wgh_ref, wgg_ref)
    norms = (hgn_ref, gln_ref)
    branch_w = (wbh_ref, wbg_ref)
    streams = [dict(rows=slice(i * rows_per_stream, (i + 1) * rows_per_stream)) for i in range(n_streams)]

    def mixer_input(t):
        t['x'] = x_ref[0, t['rows'], :]
        t['h1'] = (_rms(t['x']) * pre1_ref[...] * (1.0 + m[1]) + m[0]).astype(BF16)

    def gate_proj(t):
        t['out_gate'] = [_silu(_dot(t['h1'], gates[br][...])) for br in range(2)]
        t['merge_gate'] = _sigmoid(_dot(t['h1'], wgm_ref[...]))

    def branch_proj(t):
        o = oi_ref[0, t['rows'], :].astype(F32) + os_ref[0, t['rows'], :]
        t['ys'] = []
        for br in range(2):
            parts = []
            for hd in range(HEADS):
                c0 = br * BRANCH_W + hd * HEAD_DIM
                parts.append(_rms(o[:, c0:c0 + HEAD_DIM]) * norms[br][...])
            og = (jnp.concatenate(parts, axis=-1) * t['out_gate'][br]).astype(BF16)
            t['ys'].append(_dot(og, branch_w[br][...]))

    def out_proj(t):
        gm = t['merge_gate']
        merged = gm[:, :d_model] * t['ys'][0] + gm[:, d_model:] * t['ys'][1]
        t['y'] = _dot(merged.astype(BF16), wout_ref[...])

    def ffn_input(t):
        t['z1'] = t['x'] + _rms(t['y']) * post1_ref[...] * m[2]
        t['h2'] = (_rms(t['z1']) * pre2_ref[...] * (1.0 + m[4]) + m[3]).astype(BF16)

    def ffn_up(t):
        g = _dot(t['h2'], wg_ref[...])
        u = _dot(t['h2'], wu_ref[...])
        t['a'] = (_silu(g) * u).astype(BF16)

    def ffn_down(t):
        t['y2'] = _dot(t['a'], wd_ref[...])

    def residual_out(t):
        out_ref[0, t['rows'], :] = t['z1'] + _rms(t['y2']) * post2_ref[...] * m[5]

    for stage in (mixer_input, gate_proj, branch_proj, out_proj, ffn_input, ffn_up, ffn_down, residual_out):
        for t in streams:
            stage(t)


def _tail(oi, o_scan, x, mod3, *consts, tm):
    b, seq, d = x.shape
    kern = functools.partial(_tail_kernel, d_model=d, n_streams=TAIL_STREAMS)
    return pl.pallas_call(
        kern,
        grid=(b, seq // tm),
        in_specs=[
            pl.BlockSpec((1, tm, DIR_W), lambda i, j: (i, j, 0)),
            pl.BlockSpec((1, tm, DIR_W), lambda i, j: (i, j, 0)),
            pl.BlockSpec((1, tm, d), lambda i, j: (i, j, 0)),
            pl.BlockSpec((1, 1, N_MOD * d), lambda i, j: (i + 1, 0, 0)),
        ] + [_const_spec(c.shape) for c in consts],
        out_specs=pl.BlockSpec((1, tm, d), lambda i, j: (i, j, 0)),
        out_shape=jax.ShapeDtypeStruct((b, seq, d), F32),
        compiler_params=pltpu.CompilerParams(
            dimension_semantics=("arbitrary", "arbitrary"),
            vmem_limit_bytes=VMEM_LIMIT_BYTES),
        name="mixer_out_ffn",
    )(oi, o_scan, x, mod3, *consts)


def kernel(x, c, ctx, c_ctx, w_mod, b_mod, norm_pre1, norm_post1, norm_pre2, norm_post2, w_in, hg_lb,
           hg_onorm, gla_w_gk, gla_b_gk, gla_onorm, w_br_hg, w_br_gla, w_out, w_ff_gate, w_ff_up, w_ff_down):
    b, seq, d = x.shape
    ctx_len = ctx.shape[1]
    assert w_mod.shape[0] == 1 and d == DIR_W
    assert seq % CHUNK == 0 and ctx_len % CHUNK == 0
    tm = ctx_len

    c_all = jnp.zeros((MOD_ROWS, d), F32).at[0].set(c_ctx).at[1:b + 1].set(c)
    mod3 = _modulation(c_all, w_mod.reshape(d, N_MOD * d), b_mod).reshape(MOD_ROWS, 1, N_MOD * d)

    def bf(a):
        return a.reshape(a.shape[1:]).astype(BF16)

    w = bf(w_in)
    lr_w = 2 * GLA_RANK
    wgk = jnp.zeros((LR_PAD, 2 * BRANCH_W), F32)
    wgk = wgk.at[:GLA_RANK, :BRANCH_W].set(gla_w_gk[0, 0]).at[GLA_RANK:lr_w, BRANCH_W:].set(gla_w_gk[0, 1])
    bgk = gla_b_gk.reshape(1, 2 * BRANCH_W)

    qd, kd, v, el, oi = _in_projection(
        x, ctx, mod3, norm_pre1, w, hg_lb, wgk.astype(BF16), bgk, tm=tm)
    o_scan = _state_scan(qd, kd, v, el, ctx_len=ctx_len, seq=seq, group=tm // CHUNK)
    return _tail(oi, o_scan, x, mod3, hg_onorm, gla_onorm, norm_pre1, norm_post1, norm_pre2, norm_post2,
                 w[:, G_HGATE * BRANCH_W:(G_HGATE + 1) * BRANCH_W],
                 w[:, G_GGATE * BRANCH_W:(G_GGATE + 1) * BRANCH_W],
                 w[:, LR_COL + lr_w:],
                 bf(w_br_hg), bf(w_br_gla), bf(w_out), bf(w_ff_gate), bf(w_ff_up), bf(w_ff_down),
                 tm=TAIL_ROWS)
```

```python
import functools

import jax
import jax.numpy as jnp
from jax import lax
from jax.experimental import pallas as pl
from jax.experimental.pallas import tpu as pltpu

F32 = jnp.float32
BF16 = jnp.bfloat16

EPS = 1e-6
LOG2E = 1.4426950408889634
CHUNK = 64
N_MOD = 6
HEADS = 4
HEAD_DIM = 128
BRANCH_W = HEADS * HEAD_DIM
DIR_W = 2 * BRANCH_W
GLA_RANK = 16
GLA_GATE_NORM = 16.0
LR_PAD = 128
G_HQ, G_HI, G_HF, G_HB, G_HGATE, G_GQ, G_GK, G_GV, G_GGATE = range(9)
LR_COL = 9 * BRANCH_W
MOD_ROWS = 16
VMEM_LIMIT_BYTES = 56 * 1024 * 1024
TAIL_ROWS = 512
TAIL_STREAMS = 2
EXCURSION_LIMIT_LOG2 = 96.0


def _dot(a, b):
    return jnp.dot(a, b, preferred_element_type=F32)


def _dot_nt(a, b):
    return lax.dot_general(a, b, (((1,), (1,)), ((), ())), preferred_element_type=F32)


def _dot_tn(a, b):
    return lax.dot_general(a, b, (((0,), (0,)), ((), ())), preferred_element_type=F32)


def _sigmoid(x):
    return 1.0 / (1.0 + jnp.exp(-x))


def _silu(x):
    return x * _sigmoid(x)


def _rms(x):
    return x * lax.rsqrt(jnp.mean(x * x, axis=-1, keepdims=True) + EPS)


def _const_spec(shape):
    nd = len(shape)
    return pl.BlockSpec(shape, lambda *_: (0,) * nd, pipeline_mode=pl.Buffered(1))


def _mod_kernel(c_ref, w_ref, b_ref, o_ref):
    a = _silu(c_ref[...]).astype(BF16)
    o_ref[...] = _dot(a, w_ref[...].astype(BF16)) + b_ref[...]


def _modulation(c_all, w_mod, b_mod):
    d = c_all.shape[1]
    n = w_mod.shape[1]
    blk = 1024
    return pl.pallas_call(
        _mod_kernel,
        grid=(n // blk,),
        in_specs=[pl.BlockSpec((MOD_ROWS, d), lambda j: (0, 0)),
                  pl.BlockSpec((d, blk), lambda j: (0, j)),
                  pl.BlockSpec((1, blk), lambda j: (0, j))],
        out_specs=pl.BlockSpec((MOD_ROWS, blk), lambda j: (0, j)),
        out_shape=jax.ShapeDtypeStruct((MOD_ROWS, n), F32),
        name="modulation",
    )(c_all, w_mod, b_mod)


def _inproj_kernel(x_ref, ctx_ref, mod_ref, pre_ref, w_ref, lb_ref, wgk_ref, bgk_ref,
                   qd_ref, kd_ref, v_ref, el_ref, oi_ref, rng_ref, *scratch, d_model, tm, exact):
    n_ch = tm // CHUNK
    mids = (CHUNK // 2, CHUNK // 2 - 1)
    lasts = (CHUNK - 1, 0)

    def body(z_ref, latent):
        mod = mod_ref[0]
        shift, scale = mod[:, :d_model], mod[:, d_model:2 * d_model]
        h = (_rms(z_ref[0]) * pre_ref[...] * (1.0 + scale) + shift).astype(BF16)

        lbr = lb_ref[...]
        e = jnp.exp(lbr - jnp.max(lbr, axis=0, keepdims=True))
        lb = e[0] / jnp.sum(e, axis=0)

        row = lax.broadcasted_iota(jnp.int32, (tm, tm), 0)
        col = lax.broadcasted_iota(jnp.int32, (tm, tm), 1)
        shift_bits = CHUNK.bit_length() - 1
        same = jnp.right_shift(row, shift_bits) == jnp.right_shift(col, shift_bits)
        masks = (same & (col <= row), same & (col >= row))
        tris = tuple(jnp.where(m, 1.0, 0.0).astype(BF16) for m in masks)

        def proj(g):
            return _dot(h, w_ref[:, g * BRANCH_W:(g + 1) * BRANCH_W])

        span = []

        def prep(q, k, lf, d, col0):
            lf2 = lf * LOG2E
            hi = lf2.astype(BF16)
            md = (lf2 - hi.astype(F32)).astype(BF16)
            c = _dot(tris[d], hi) + _dot(tris[d], md)
            qts, kts = [], []
            for n in range(n_ch):
                rows = slice(n * CHUNK, (n + 1) * CHUNK)
                cj = c[rows]
                cl = cj[lasts[d]:lasts[d] + 1]
                el_ref[0, n, :, col0:col0 + BRANCH_W] = jnp.exp2(cl)
                if exact or not latent:
                    kd_ref[0, rows, col0:col0 + BRANCH_W] = (k[rows] * jnp.exp2(cl - cj)).astype(BF16)
                    if latent:
                        qd_ref[0, rows, col0:col0 + BRANCH_W] = (q[rows] * jnp.exp2(cj)).astype(BF16)
                    continue
                r = cj[mids[d]:mids[d] + 1]
                dq = cj - r
                span.append(jnp.maximum(jnp.abs(dq[0:1]), jnp.abs(dq[CHUNK - 1:CHUNK])))
                qt = q[rows] * jnp.exp2(dq)
                kt = k[rows] * jnp.exp2(-dq)
                qd_ref[0, rows, col0:col0 + BRANCH_W] = (qt * jnp.exp2(r)).astype(BF16)
                kd_ref[0, rows, col0:col0 + BRANCH_W] = (kt * jnp.exp2(cl - r)).astype(BF16)
                qts.append(qt.astype(BF16))
                kts.append(kt.astype(BF16))
            if not latent:
                return None
            if exact:
                return q, k, c
            return jnp.concatenate(qts, axis=0), jnp.concatenate(kts, axis=0)

        def intra_exact(qkc, v, d):
            q, k, c = qkc
            k_s, v_s, c_s = scratch
            k_s[...] = k
            v_s[...] = v.astype(F32)
            c_s[...] = c
            rowid = lax.broadcasted_iota(jnp.int32, (CHUNK, 1), 0)
            outs = []
            for n in range(n_ch):
                base = n * CHUNK
                qj = q[base:base + CHUNK]
                cj = c[base:base + CHUNK]

                def step(s, acc):
                    cs = c_s[pl.ds(base + s, 1), :]
                    ks = k_s[pl.ds(base + s, 1), :]
                    vs = v_s[pl.ds(base + s, 1), :]
                    at_or_after = (rowid >= s) if d == 0 else (rowid <= s)
                    p = jnp.where(at_or_after, qj * jnp.exp2(jnp.minimum(cj - cs, 0.0)) * ks, 0.0)
                    parts = []
                    for hd in range(HEADS):
                        sl = slice(hd * HEAD_DIM, (hd + 1) * HEAD_DIM)
                        parts.append(jnp.sum(p[:, sl], axis=-1, keepdims=True) * vs[:, sl])
                    return acc + jnp.concatenate(parts, axis=-1)

                outs.append(lax.fori_loop(0, CHUNK, step, jnp.zeros((CHUNK, BRANCH_W), F32)))
            return jnp.concatenate(outs, axis=0)

        def intra(qk, v, d):
            if not latent:
                return None
            if exact:
                return intra_exact(qk, v, d)
            qt, kt = qk
            outs = []
            for hd in range(HEADS):
                sl = slice(hd * HEAD_DIM, (hd + 1) * HEAD_DIM)
                a = _dot_nt(qt[:, sl], kt[:, sl])
                a = jnp.where(masks[d], a, 0.0).astype(BF16)
                outs.append(_dot(a, v[:, sl]))
            return jnp.concatenate(outs, axis=-1)

        def hg_gate(raw, d):
            lbd = lb[d:d + 1]
            f = lbd + (1.0 - lbd) * _sigmoid(raw)
            return 1.0 - f, jnp.log(f)

        pq_h = proj(G_HQ) if latent else None
        pv_h = proj(G_HI)
        pf = [proj(G_HF + d) for d in range(2)]
        q_h = _silu(pq_h) if latent else None
        v_h = pv_h.astype(BF16)
        v_ref[0, :, 0:BRANCH_W] = v_h
        qk_h0 = prep(q_h, *hg_gate(pf[0], 0), 0, 0)
        pq_g = proj(G_GQ) if latent else None
        pk_g = proj(G_GK)
        pv_g = proj(G_GV)
        lr = _dot(h, w_ref[:, LR_COL:LR_COL + LR_PAD]).astype(BF16)
        xg = _dot(lr, wgk_ref[...]) + bgk_ref[...]
        qk_h1 = prep(q_h, *hg_gate(pf[1], 1), 1, DIR_W)
        o0 = intra(qk_h0, v_h, 0)
        o1 = intra(qk_h1, v_h, 1)
        q_g = pq_g * HEAD_DIM ** -0.5 if latent else None
        v_g = pv_g.astype(BF16)
        v_ref[0, :, BRANCH_W:2 * BRANCH_W] = v_g
        ls = (jnp.minimum(xg, 0.0) - jnp.log(1.0 + jnp.exp(-jnp.abs(xg)))) * (1.0 / GLA_GATE_NORM)
        qk_g0 = prep(q_g, pk_g, ls[:, :BRANCH_W], 0, BRANCH_W)
        if latent:
            oi_ref[0, :, 0:BRANCH_W] = (o0 + o1).astype(BF16)
        qk_g1 = prep(q_g, pk_g, ls[:, BRANCH_W:], 1, DIR_W + BRANCH_W)
        o0 = intra(qk_g0, v_g, 0)
        o1 = intra(qk_g1, v_g, 1)
        if latent:
            oi_ref[0, :, BRANCH_W:2 * BRANCH_W] = (o0 + o1).astype(BF16)

        widest = jnp.zeros((1, BRANCH_W), F32)
        for piece in span:
            widest = jnp.maximum(widest, piece)
        lanes = widest[:, 0:HEAD_DIM]
        for hd in range(1, HEADS):
            lanes = jnp.maximum(lanes, widest[:, hd * HEAD_DIM:(hd + 1) * HEAD_DIM])
        rng_ref[0, 0] = lanes

    j = pl.program_id(1)

    @pl.when(j == 0)
    def _():
        body(ctx_ref, False)

    @pl.when(j > 0)
    def _():
        body(x_ref, True)


def _in_projection(x, ctx, mod3, norm_pre, w_in_r, hg_lb, wgk, bgk, *, tm, exact):
    b, seq, d = x.shape
    ctx_len = ctx.shape[1]
    assert ctx_len == tm and tm % CHUNK == 0
    t_all = ctx_len + seq
    n_lat = seq // tm
    n_ch = tm // CHUNK
    nw = w_in_r.shape[1]
    kern = functools.partial(_inproj_kernel, d_model=d, tm=tm, exact=exact)

    def all_rows(w):
        return pl.BlockSpec((1, tm, w), lambda i, j: (i, j, 0))

    def latent_rows(w):
        return pl.BlockSpec((1, tm, w), lambda i, j: (i, jnp.maximum(j - 1, 0), 0))

    return pl.pallas_call(
        kern,
        grid=(b, n_lat + 1),
        in_specs=[
            latent_rows(d),
            pl.BlockSpec((1, tm, d), lambda i, j: (i, 0, 0)),
            pl.BlockSpec((1, 1, N_MOD * d), lambda i, j: (jnp.where(j == 0, 0, i + 1), 0, 0)),
            _const_spec((1, d)),
            _const_spec((d, nw)),
            _const_spec(hg_lb.shape),
            _const_spec(wgk.shape),
            _const_spec(bgk.shape),
        ],
        out_specs=[
            latent_rows(2 * DIR_W),
            all_rows(2 * DIR_W),
            all_rows(DIR_W),
            pl.BlockSpec((1, n_ch, 1, 2 * DIR_W), lambda i, j: (i, j, 0, 0)),
            latent_rows(DIR_W),
            pl.BlockSpec((1, 1, 1, HEAD_DIM), lambda i, j: (i, j, 0, 0)),
        ],
        out_shape=[
            jax.ShapeDtypeStruct((b, seq, 2 * DIR_W), BF16),
            jax.ShapeDtypeStruct((b, t_all, 2 * DIR_W), BF16),
            jax.ShapeDtypeStruct((b, t_all, DIR_W), BF16),
            jax.ShapeDtypeStruct((b, t_all // CHUNK, 1, 2 * DIR_W), F32),
            jax.ShapeDtypeStruct((b, seq, DIR_W), BF16),
            jax.ShapeDtypeStruct((b, n_lat + 1, 1, HEAD_DIM), F32),
        ],
        scratch_shapes=[pltpu.VMEM((tm, BRANCH_W), F32)] * 3 if exact else [],
        compiler_params=pltpu.CompilerParams(
            dimension_semantics=("arbitrary", "arbitrary"),
            vmem_limit_bytes=VMEM_LIMIT_BYTES),
        name="in_projection_exact" if exact else "in_projection",
    )(x, ctx, mod3, norm_pre, w_in_r, hg_lb, wgk, bgk)


def _scan_kernel(qdf_ref, qdb_ref, kdf_ref, kdb_ref, vf_ref, vb_ref, elf_ref, elb_ref,
                 o_ref, s_ref, *, ctx_steps, n_steps, group):
    s = pl.program_id(1)
    rows_per_step = group * CHUNK

    @pl.when(s == 0)
    def _():
        s_ref[...] = jnp.zeros_like(s_ref)
        o_ref[...] = jnp.zeros_like(o_ref)

    def run(with_out):
        if with_out:
            base = (pl.multiple_of((s - ctx_steps) * rows_per_step, rows_per_step),
                    pl.multiple_of((n_steps - 1 - s) * rows_per_step, rows_per_step))
        for g in range(group):
            for d, (qd_ref, kd_ref, v_ref, el_ref) in enumerate((
                    (qdf_ref, kdf_ref, vf_ref, elf_ref), (qdb_ref, kdb_ref, vb_ref, elb_ref))):
                ci = g if d == 0 else group - 1 - g
                rows = slice(ci * CHUNK, (ci + 1) * CHUNK)
                e_last = el_ref[0, ci]
                outs = []
                for u in range(2 * HEADS):
                    sl = slice(u * HEAD_DIM, (u + 1) * HEAD_DIM)
                    st = s_ref[d * 2 * HEADS + u]
                    if with_out:
                        outs.append(_dot_nt(qd_ref[0, rows, sl], st.astype(BF16)))
                    s_ref[d * 2 * HEADS + u] = (st * e_last[:, sl]
                                                + _dot_tn(v_ref[0, rows, sl], kd_ref[0, rows, sl]))
                if with_out:
                    o_ref[0, pl.ds(base[d] + ci * CHUNK, CHUNK), :] += jnp.concatenate(outs, axis=-1)

    @pl.when(s < ctx_steps)
    def _():
        run(False)

    @pl.when(s >= ctx_steps)
    def _():
        run(True)


def _state_scan(qd, kd, v, el, *, ctx_len, seq, group):
    b = qd.shape[0]
    rows_per_step = group * CHUNK
    assert ctx_len % rows_per_step == 0 and seq % rows_per_step == 0
    ctx_steps = ctx_len // rows_per_step
    n_steps = (ctx_len + seq) // rows_per_step

    def bw_block(s):
        return jnp.where(s < ctx_steps, ctx_steps - 1 - s, n_steps + ctx_steps - 1 - s)

    def fw_rows(i, s):
        return (i, s, 0)

    def bw_rows(i, s):
        return (i, bw_block(s), 1)

    def bw_rows_v(i, s):
        return (i, bw_block(s), 0)

    def fw_lat(i, s):
        return (i, jnp.maximum(s, ctx_steps) - ctx_steps, 0)

    def bw_lat(i, s):
        return (i, n_steps - 1 - jnp.maximum(s, ctx_steps), 1)

    kern = functools.partial(_scan_kernel, ctx_steps=ctx_steps, n_steps=n_steps, group=group)
    blk = (1, rows_per_step, DIR_W)
    eblk = (1, group, 1, DIR_W)
    return pl.pallas_call(
        kern,
        grid=(b, n_steps),
        in_specs=[pl.BlockSpec(blk, fw_lat), pl.BlockSpec(blk, bw_lat),
                  pl.BlockSpec(blk, fw_rows), pl.BlockSpec(blk, bw_rows),
                  pl.BlockSpec(blk, fw_rows), pl.BlockSpec(blk, bw_rows_v),
                  pl.BlockSpec(eblk, lambda i, s: (i, s, 0, 0)),
                  pl.BlockSpec(eblk, lambda i, s: (i, bw_block(s), 0, 1))],
        out_specs=pl.BlockSpec((1, seq, DIR_W), lambda i, s: (i, 0, 0)),
        out_shape=jax.ShapeDtypeStruct((b, seq, DIR_W), F32),
        scratch_shapes=[pltpu.VMEM((4 * HEADS, HEAD_DIM, HEAD_DIM), F32)],
        compiler_params=pltpu.CompilerParams(
            dimension_semantics=("arbitrary", "arbitrary"),
            vmem_limit_bytes=VMEM_LIMIT_BYTES),
        name="state_scan",
    )(qd, qd, kd, kd, v, v, el, el)


def _tail_kernel(oi_ref, os_ref, x_ref, mod_ref, hgn_ref, gln_ref, pre1_ref, post1_ref, pre2_ref, post2_ref,
                 wgh_ref, wgg_ref, wgm_ref, wbh_ref, wbg_ref, wout_ref, wg_ref, wu_ref, wd_ref, out_ref,
                 *, d_model, n_streams):
    mod = mod_ref[0]
    m = [mod[:, i * d_model:(i + 1) * d_model] for i in range(N_MOD)]
    rows_per_stream = x_ref.shape[1] // n_streams
    gates = (wgh_ref, wgg_ref)
    norms = (hgn_ref, gln_ref)
    branch_w = (wbh_ref, wbg_ref)
    streams = [dict(rows=slice(i * rows_per_stream, (i + 1) * rows_per_stream)) for i in range(n_streams)]

    def mixer_input(t):
        t['x'] = x_ref[0, t['rows'], :]
        t['h1'] = (_rms(t['x']) * pre1_ref[...] * (1.0 + m[1]) + m[0]).astype(BF16)

    def gate_proj(t):
        t['out_gate'] = [_silu(_dot(t['h1'], gates[br][...])) for br in range(2)]
        t['merge_gate'] = _sigmoid(_dot(t['h1'], wgm_ref[...]))

    def branch_proj(t):
        o = oi_ref[0, t['rows'], :].astype(F32) + os_ref[0, t['rows'], :]
        t['ys'] = []
        for br in range(2):
            parts = []
            for hd in range(HEADS):
                c0 = br * BRANCH_W + hd * HEAD_DIM
                parts.append(_rms(o[:, c0:c0 + HEAD_DIM]) * norms[br][...])
            og = (jnp.concatenate(parts, axis=-1) * t['out_gate'][br]).astype(BF16)
            t['ys'].append(_dot(og, branch_w[br][...]))

    def out_proj(t):
        gm = t['merge_gate']
        merged = gm[:, :d_model] * t['ys'][0] + gm[:, d_model:] * t['ys'][1]
        t['y'] = _dot(merged.astype(BF16), wout_ref[...])

    def ffn_input(t):
        t['z1'] = t['x'] + _rms(t['y']) * post1_ref[...] * m[2]
        t['h2'] = (_rms(t['z1']) * pre2_ref[...] * (1.0 + m[4]) + m[3]).astype(BF16)

    def ffn_up(t):
        g = _dot(t['h2'], wg_ref[...])
        u = _dot(t['h2'], wu_ref[...])
        t['a'] = (_silu(g) * u).astype(BF16)

    def ffn_down(t):
        t['y2'] = _dot(t['a'], wd_ref[...])

    def residual_out(t):
        out_ref[0, t['rows'], :] = t['z1'] + _rms(t['y2']) * post2_ref[...] * m[5]

    for stage in (mixer_input, gate_proj, branch_proj, out_proj, ffn_input, ffn_up, ffn_down, residual_out):
        for t in streams:
            stage(t)


def _tail(oi, o_scan, x, mod3, *consts, tm):
    b, seq, d = x.shape
    kern = functools.partial(_tail_kernel, d_model=d, n_streams=TAIL_STREAMS)
    return pl.pallas_call(
        kern,
        grid=(b, seq // tm),
        in_specs=[
            pl.BlockSpec((1, tm, DIR_W), lambda i, j: (i, j, 0)),
            pl.BlockSpec((1, tm, DIR_W), lambda i, j: (i, j, 0)),
            pl.BlockSpec((1, tm, d), lambda i, j: (i, j, 0)),
            pl.BlockSpec((1, 1, N_MOD * d), lambda i, j: (i + 1, 0, 0)),
        ] + [_const_spec(c.shape) for c in consts],
        out_specs=pl.BlockSpec((1, tm, d), lambda i, j: (i, j, 0)),
        out_shape=jax.ShapeDtypeStruct((b, seq, d), F32),
        compiler_params=pltpu.CompilerParams(
            dimension_semantics=("arbitrary", "arbitrary"),
            vmem_limit_bytes=VMEM_LIMIT_BYTES),
        name="mixer_out_ffn",
    )(oi, o_scan, x, mod3, *consts)


def kernel(x, c, ctx, c_ctx, w_mod, b_mod, norm_pre1, norm_post1, norm_pre2, norm_post2, w_in, hg_lb,
           hg_onorm, gla_w_gk, gla_b_gk, gla_onorm, w_br_hg, w_br_gla, w_out, w_ff_gate, w_ff_up, w_ff_down):
    b, seq, d = x.shape
    ctx_len = ctx.shape[1]
    assert w_mod.shape[0] == 1 and d == DIR_W
    assert seq % CHUNK == 0 and ctx_len % CHUNK == 0
    tm = ctx_len

    c_all = jnp.zeros((MOD_ROWS, d), F32).at[0].set(c_ctx).at[1:b + 1].set(c)
    mod3 = _modulation(c_all, w_mod.reshape(d, N_MOD * d), b_mod).reshape(MOD_ROWS, 1, N_MOD * d)

    def bf(a):
        return a.reshape(a.shape[1:]).astype(BF16)

    w = bf(w_in)
    lr_w = 2 * GLA_RANK
    wgk = jnp.zeros((LR_PAD, 2 * BRANCH_W), F32)
    wgk = wgk.at[:GLA_RANK, :BRANCH_W].set(gla_w_gk[0, 0]).at[GLA_RANK:lr_w, BRANCH_W:].set(gla_w_gk[0, 1])
    bgk = gla_b_gk.reshape(1, 2 * BRANCH_W)

    def in_projection(exact):
        return _in_projection(x, ctx, mod3, norm_pre1, w, hg_lb, wgk.astype(BF16), bgk, tm=tm, exact=exact)

    def rest(qd, kd, v, el, oi):
        o_scan = _state_scan(qd, kd, v, el, ctx_len=ctx_len, seq=seq, group=tm // CHUNK)
        return _tail(oi, o_scan, x, mod3, hg_onorm, gla_onorm, norm_pre1, norm_post1, norm_pre2, norm_post2,
                     w[:, G_HGATE * BRANCH_W:(G_HGATE + 1) * BRANCH_W],
                     w[:, G_GGATE * BRANCH_W:(G_GGATE + 1) * BRANCH_W],
                     w[:, LR_COL + lr_w:],
                     bf(w_br_hg), bf(w_br_gla), bf(w_out), bf(w_ff_gate), bf(w_ff_up), bf(w_ff_down),
                     tm=TAIL_ROWS)

    *fast, excursion = in_projection(False)
    return lax.cond(jnp.max(excursion) > EXCURSION_LIMIT_LOG2,
                    lambda: rest(*in_projection(True)[:5]),
                    lambda: rest(*fast))
```

```python
import functools

import jax
import jax.numpy as jnp
from jax import lax
from jax.experimental import pallas as pl
from jax.experimental.pallas import tpu as pltpu

F32 = jnp.float32
BF16 = jnp.bfloat16

EPS = 1e-6
LOG2E = 1.4426950408889634
CHUNK = 64
N_MOD = 6
HEADS = 4
HEAD_DIM = 128
BRANCH_W = HEADS * HEAD_DIM
DIR_W = 2 * BRANCH_W
GLA_RANK = 16
GLA_GATE_NORM = 16.0
LR_PAD = 128
G_HQ, G_HI, G_HF, G_HB, G_HGATE, G_GQ, G_GK, G_GV, G_GGATE = range(9)
LR_COL = 9 * BRANCH_W
MOD_ROWS = 16
VMEM_LIMIT_BYTES = 56 * 1024 * 1024
TAIL_ROWS = 512
TAIL_STREAMS = 2
CAST_SLAB_ROWS = 16
EXCURSION_LIMIT_LOG2 = 96.0


def _dot(a, b):
    return jnp.dot(a, b, preferred_element_type=F32)


def _dot_nt(a, b):
    return lax.dot_general(a, b, (((1,), (1,)), ((), ())), preferred_element_type=F32)


def _dot_tn(a, b):
    return lax.dot_general(a, b, (((0,), (0,)), ((), ())), preferred_element_type=F32)


def _sigmoid(x):
    return 1.0 / (1.0 + jnp.exp(-x))


def _silu(x):
    return x * _sigmoid(x)


def _rms(x):
    return x * lax.rsqrt(jnp.mean(x * x, axis=-1, keepdims=True) + EPS)


def _const_spec(shape):
    nd = len(shape)
    return pl.BlockSpec(shape, lambda *_: (0,) * nd, pipeline_mode=pl.Buffered(1))


def _mod_kernel(c_ref, w_ref, b_ref, o_ref):
    a = _silu(c_ref[...]).astype(BF16)
    o_ref[...] = _dot(a, w_ref[...].astype(BF16)) + b_ref[...]


def _modulation(c_all, w_mod, b_mod):
    d = c_all.shape[1]
    n = w_mod.shape[1]
    blk = 1024
    return pl.pallas_call(
        _mod_kernel,
        grid=(n // blk,),
        in_specs=[pl.BlockSpec((MOD_ROWS, d), lambda j: (0, 0)),
                  pl.BlockSpec((d, blk), lambda j: (0, j)),
                  pl.BlockSpec((1, blk), lambda j: (0, j))],
        out_specs=pl.BlockSpec((MOD_ROWS, blk), lambda j: (0, j)),
        out_shape=jax.ShapeDtypeStruct((MOD_ROWS, n), F32),
        name="modulation",
    )(c_all, w_mod, b_mod)


def _inproj_kernel(x_ref, ctx_ref, mod_ref, pre_ref, w_ref, lb_ref, wgk_ref, bgk_ref, *rest,
                   d_model, tm, exact, n_cast):
    cast_in, rest = rest[:n_cast], rest[n_cast:]
    qd_ref, kd_ref, v_ref, el_ref, oi_ref, rng_ref = rest[:6]
    cast_out, scratch = rest[6:6 + n_cast], rest[6 + n_cast:]
    n_ch = tm // CHUNK
    mids = (CHUNK // 2, CHUNK // 2 - 1)
    lasts = (CHUNK - 1, 0)

    def body(z_ref, latent):
        if latent:
            for wi_ref, wo_ref in zip(cast_in, cast_out):
                wo_ref[...] = wi_ref[...].astype(BF16)
        mod = mod_ref[0]
        shift, scale = mod[:, :d_model], mod[:, d_model:2 * d_model]
        h = (_rms(z_ref[0]) * pre_ref[...] * (1.0 + scale) + shift).astype(BF16)

        lbr = lb_ref[...]
        e = jnp.exp(lbr - jnp.max(lbr, axis=0, keepdims=True))
        lb = e[0] / jnp.sum(e, axis=0)

        row = lax.broadcasted_iota(jnp.int32, (tm, tm), 0)
        col = lax.broadcasted_iota(jnp.int32, (tm, tm), 1)
        shift_bits = CHUNK.bit_length() - 1
        same = jnp.right_shift(row, shift_bits) == jnp.right_shift(col, shift_bits)
        masks = (same & (col <= row), same & (col >= row))
        tris = tuple(jnp.where(m, 1.0, 0.0).astype(BF16) for m in masks)

        def proj(g):
            return _dot(h, w_ref[:, g * BRANCH_W:(g + 1) * BRANCH_W])

        span = []

        def prep(q, k, lf, d, col0):
            lf2 = lf * LOG2E
            hi = lf2.astype(BF16)
            md = (lf2 - hi.astype(F32)).astype(BF16)
            c = _dot(tris[d], hi) + _dot(tris[d], md)
            qts, kts = [], []
            for n in range(n_ch):
                rows = slice(n * CHUNK, (n + 1) * CHUNK)
                cj = c[rows]
                cl = cj[lasts[d]:lasts[d] + 1]
                el_ref[0, n, :, col0:col0 + BRANCH_W] = jnp.exp2(cl)
                if exact or not latent:
                    kd_ref[0, rows, col0:col0 + BRANCH_W] = (k[rows] * jnp.exp2(cl - cj)).astype(BF16)
                    if latent:
                        qd_ref[0, rows, col0:col0 + BRANCH_W] = (q[rows] * jnp.exp2(cj)).astype(BF16)
                    continue
                r = cj[mids[d]:mids[d] + 1]
                dq = cj - r
                span.append(jnp.maximum(jnp.abs(dq[0:1]), jnp.abs(dq[CHUNK - 1:CHUNK])))
                qt = q[rows] * jnp.exp2(dq)
                kt = k[rows] * jnp.exp2(-dq)
                qd_ref[0, rows, col0:col0 + BRANCH_W] = (qt * jnp.exp2(r)).astype(BF16)
                kd_ref[0, rows, col0:col0 + BRANCH_W] = (kt * jnp.exp2(cl - r)).astype(BF16)
                qts.append(qt.astype(BF16))
                kts.append(kt.astype(BF16))
            if not latent:
                return None
            if exact:
                return q, k, c
            return jnp.concatenate(qts, axis=0), jnp.concatenate(kts, axis=0)

        def intra_exact(qkc, v, d):
            q, k, c = qkc
            k_s, v_s, c_s = scratch
            k_s[...] = k
            v_s[...] = v.astype(F32)
            c_s[...] = c
            rowid = lax.broadcasted_iota(jnp.int32, (CHUNK, 1), 0)
            outs = []
            for n in range(n_ch):
                base = n * CHUNK
                qj = q[base:base + CHUNK]
                cj = c[base:base + CHUNK]

                def step(s, acc):
                    cs = c_s[pl.ds(base + s, 1), :]
                    ks = k_s[pl.ds(base + s, 1), :]
                    vs = v_s[pl.ds(base + s, 1), :]
                    at_or_after = (rowid >= s) if d == 0 else (rowid <= s)
                    p = jnp.where(at_or_after, qj * jnp.exp2(jnp.minimum(cj - cs, 0.0)) * ks, 0.0)
                    parts = []
                    for hd in range(HEADS):
                        sl = slice(hd * HEAD_DIM, (hd + 1) * HEAD_DIM)
                        parts.append(jnp.sum(p[:, sl], axis=-1, keepdims=True) * vs[:, sl])
                    return acc + jnp.concatenate(parts, axis=-1)

                outs.append(lax.fori_loop(0, CHUNK, step, jnp.zeros((CHUNK, BRANCH_W), F32)))
            return jnp.concatenate(outs, axis=0)

        def intra(qk, v, d):
            if not latent:
                return None
            if exact:
                return intra_exact(qk, v, d)
            qt, kt = qk
            outs = []
            for hd in range(HEADS):
                sl = slice(hd * HEAD_DIM, (hd + 1) * HEAD_DIM)
                a = _dot_nt(qt[:, sl], kt[:, sl])
                a = jnp.where(masks[d], a, 0.0).astype(BF16)
                outs.append(_dot(a, v[:, sl]))
            return jnp.concatenate(outs, axis=-1)

        def hg_gate(raw, d):
            lbd = lb[d:d + 1]
            f = lbd + (1.0 - lbd) * _sigmoid(raw)
            return 1.0 - f, jnp.log(f)

        pq_h = proj(G_HQ) if latent else None
        pv_h = proj(G_HI)
        pf = [proj(G_HF + d) for d in range(2)]
        q_h = _silu(pq_h) if latent else None
        v_h = pv_h.astype(BF16)
        v_ref[0, :, 0:BRANCH_W] = v_h
        qk_h0 = prep(q_h, *hg_gate(pf[0], 0), 0, 0)
        pq_g = proj(G_GQ) if latent else None
        pk_g = proj(G_GK)
        pv_g = proj(G_GV)
        lr = _dot(h, w_ref[:, LR_COL:LR_COL + LR_PAD]).astype(BF16)
        xg = _dot(lr, wgk_ref[...]) + bgk_ref[...]
        qk_h1 = prep(q_h, *hg_gate(pf[1], 1), 1, DIR_W)
        o0 = intra(qk_h0, v_h, 0)
        o1 = intra(qk_h1, v_h, 1)
        q_g = pq_g * HEAD_DIM ** -0.5 if latent else None
        v_g = pv_g.astype(BF16)
        v_ref[0, :, BRANCH_W:2 * BRANCH_W] = v_g
        ls = (jnp.minimum(xg, 0.0) - jnp.log(1.0 + jnp.exp(-jnp.abs(xg)))) * (1.0 / GLA_GATE_NORM)
        qk_g0 = prep(q_g, pk_g, ls[:, :BRANCH_W], 0, BRANCH_W)
        if latent:
            oi_ref[0, :, 0:BRANCH_W] = (o0 + o1).astype(BF16)
        qk_g1 = prep(q_g, pk_g, ls[:, BRANCH_W:], 1, DIR_W + BRANCH_W)
        o0 = intra(qk_g0, v_g, 0)
        o1 = intra(qk_g1, v_g, 1)
        if latent:
            oi_ref[0, :, BRANCH_W:2 * BRANCH_W] = (o0 + o1).astype(BF16)

        widest = jnp.zeros((1, BRANCH_W), F32)
        for piece in span:
            widest = jnp.maximum(widest, piece)
        lanes = widest[:, 0:HEAD_DIM]
        for hd in range(1, HEADS):
            lanes = jnp.maximum(lanes, widest[:, hd * HEAD_DIM:(hd + 1) * HEAD_DIM])
        rng_ref[0, 0] = lanes

    j = pl.program_id(1)

    @pl.when(j == 0)
    def _():
        body(ctx_ref, False)

    @pl.when(j > 0)
    def _():
        body(x_ref, True)


def _in_projection(x, ctx, mod3, norm_pre, w_in_r, hg_lb, wgk, bgk, cast_weights, *, tm, exact):
    b, seq, d = x.shape
    ctx_len = ctx.shape[1]
    assert ctx_len == tm and tm % CHUNK == 0
    t_all = ctx_len + seq
    n_lat = seq // tm
    n_ch = tm // CHUNK
    nw = w_in_r.shape[1]
    n_cast = len(cast_weights)
    slab = CAST_SLAB_ROWS
    cast_2d = [cw.reshape(b * n_lat * slab, -1) for cw in cast_weights]
    kern = functools.partial(_inproj_kernel, d_model=d, tm=tm, exact=exact, n_cast=n_cast)

    def cast_spec(cw):
        return pl.BlockSpec((slab, cw.shape[1]), lambda i, j: (i * n_lat + jnp.maximum(j - 1, 0), 0))

    def all_rows(w):
        return pl.BlockSpec((1, tm, w), lambda i, j: (i, j, 0))

    def latent_rows(w):
        return pl.BlockSpec((1, tm, w), lambda i, j: (i, jnp.maximum(j - 1, 0), 0))

    return pl.pallas_call(
        kern,
        grid=(b, n_lat + 1),
        in_specs=[
            latent_rows(d),
            pl.BlockSpec((1, tm, d), lambda i, j: (i, 0, 0)),
            pl.BlockSpec((1, 1, N_MOD * d), lambda i, j: (jnp.where(j == 0, 0, i + 1), 0, 0)),
            _const_spec((1, d)),
            _const_spec((d, nw)),
            _const_spec(hg_lb.shape),
            _const_spec(wgk.shape),
            _const_spec(bgk.shape),
        ] + [cast_spec(cw) for cw in cast_2d],
        out_specs=[
            latent_rows(2 * DIR_W),
            all_rows(2 * DIR_W),
            all_rows(DIR_W),
            pl.BlockSpec((1, n_ch, 1, 2 * DIR_W), lambda i, j: (i, j, 0, 0)),
            latent_rows(DIR_W),
            pl.BlockSpec((1, 1, 1, HEAD_DIM), lambda i, j: (i, j, 0, 0)),
        ] + [cast_spec(cw) for cw in cast_2d],
        out_shape=[
            jax.ShapeDtypeStruct((b, seq, 2 * DIR_W), BF16),
            jax.ShapeDtypeStruct((b, t_all, 2 * DIR_W), BF16),
            jax.ShapeDtypeStruct((b, t_all, DIR_W), BF16),
            jax.ShapeDtypeStruct((b, t_all // CHUNK, 1, 2 * DIR_W), F32),
            jax.ShapeDtypeStruct((b, seq, DIR_W), BF16),
            jax.ShapeDtypeStruct((b, n_lat + 1, 1, HEAD_DIM), F32),
        ] + [jax.ShapeDtypeStruct(cw.shape, BF16) for cw in cast_2d],
        scratch_shapes=[pltpu.VMEM((tm, BRANCH_W), F32)] * 3 if exact else [],
        compiler_params=pltpu.CompilerParams(
            dimension_semantics=("arbitrary", "arbitrary"),
            vmem_limit_bytes=VMEM_LIMIT_BYTES),
        name="in_projection_exact" if exact else "in_projection",
    )(x, ctx, mod3, norm_pre, w_in_r, hg_lb, wgk, bgk, *cast_2d)


def _scan_kernel(qdf_ref, qdb_ref, kdf_ref, kdb_ref, vf_ref, vb_ref, elf_ref, elb_ref,
                 o_ref, s_ref, acc_ref, *, ctx_steps, n_steps, group):
    s = pl.program_id(1)
    rows_per_step = group * CHUNK

    @pl.when(s == 0)
    def _():
        s_ref[...] = jnp.zeros_like(s_ref)
        acc_ref[...] = jnp.zeros_like(acc_ref)

    def run(with_out):
        if with_out:
            base = (pl.multiple_of((s - ctx_steps) * rows_per_step, rows_per_step),
                    pl.multiple_of((n_steps - 1 - s) * rows_per_step, rows_per_step))
        for g in range(group):
            for d, (qd_ref, kd_ref, v_ref, el_ref) in enumerate((
                    (qdf_ref, kdf_ref, vf_ref, elf_ref), (qdb_ref, kdb_ref, vb_ref, elb_ref))):
                ci = g if d == 0 else group - 1 - g
                rows = slice(ci * CHUNK, (ci + 1) * CHUNK)
                e_last = el_ref[0, ci]
                outs = []
                for u in range(2 * HEADS):
                    sl = slice(u * HEAD_DIM, (u + 1) * HEAD_DIM)
                    st = s_ref[d * 2 * HEADS + u]
                    if with_out:
                        outs.append(_dot_nt(qd_ref[0, rows, sl], st.astype(BF16)))
                    s_ref[d * 2 * HEADS + u] = (st * e_last[:, sl]
                                                + _dot_tn(v_ref[0, rows, sl], kd_ref[0, rows, sl]))
                if with_out:
                    acc_ref[pl.ds(base[d] + ci * CHUNK, CHUNK), :] += jnp.concatenate(outs, axis=-1)

    @pl.when(s < ctx_steps)
    def _():
        run(False)

    @pl.when(s >= ctx_steps)
    def _():
        run(True)

    @pl.when(s == n_steps - 1)
    def _():
        o_ref[0] = acc_ref[...].astype(o_ref.dtype)


def _state_scan(qd, kd, v, el, *, ctx_len, seq, group):
    b = qd.shape[0]
    rows_per_step = group * CHUNK
    assert ctx_len % rows_per_step == 0 and seq % rows_per_step == 0
    ctx_steps = ctx_len // rows_per_step
    n_steps = (ctx_len + seq) // rows_per_step

    def bw_block(s):
        return jnp.where(s < ctx_steps, ctx_steps - 1 - s, n_steps + ctx_steps - 1 - s)

    def fw_rows(i, s):
        return (i, s, 0)

    def bw_rows(i, s):
        return (i, bw_block(s), 1)

    def bw_rows_v(i, s):
        return (i, bw_block(s), 0)

    def fw_lat(i, s):
        return (i, jnp.maximum(s, ctx_steps) - ctx_steps, 0)

    def bw_lat(i, s):
        return (i, n_steps - 1 - jnp.maximum(s, ctx_steps), 1)

    kern = functools.partial(_scan_kernel, ctx_steps=ctx_steps, n_steps=n_steps, group=group)
    blk = (1, rows_per_step, DIR_W)
    eblk = (1, group, 1, DIR_W)
    return pl.pallas_call(
        kern,
        grid=(b, n_steps),
        in_specs=[pl.BlockSpec(blk, fw_lat), pl.BlockSpec(blk, bw_lat),
                  pl.BlockSpec(blk, fw_rows), pl.BlockSpec(blk, bw_rows),
                  pl.BlockSpec(blk, fw_rows), pl.BlockSpec(blk, bw_rows_v),
                  pl.BlockSpec(eblk, lambda i, s: (i, s, 0, 0)),
                  pl.BlockSpec(eblk, lambda i, s: (i, bw_block(s), 0, 1))],
        out_specs=pl.BlockSpec((1, seq, DIR_W), lambda i, s: (i, 0, 0)),
        out_shape=jax.ShapeDtypeStruct((b, seq, DIR_W), BF16),
        scratch_shapes=[pltpu.VMEM((4 * HEADS, HEAD_DIM, HEAD_DIM), F32),
                        pltpu.VMEM((seq, DIR_W), F32)],
        compiler_params=pltpu.CompilerParams(
            dimension_semantics=("arbitrary", "arbitrary"),
            vmem_limit_bytes=VMEM_LIMIT_BYTES),
        name="state_scan",
    )(qd, qd, kd, kd, v, v, el, el)


def _tail_kernel(oi_ref, os_ref, x_ref, mod_ref, hgn_ref, gln_ref, pre1_ref, post1_ref, pre2_ref, post2_ref,
                 wgh_ref, wgg_ref, wgm_ref, wbh_ref, wbg_ref, wout_ref, wg_ref, wu_ref, wd_ref, out_ref,
                 *, d_model, n_streams):
    mod = mod_ref[0]
    m = [mod[:, i * d_model:(i + 1) * d_model] for i in range(N_MOD)]
    rows_per_stream = x_ref.shape[1] // n_streams
    gates = (wgh_ref, wgg_ref)
    norms = (hgn_ref, gln_ref)
    branch_w = (wbh_ref, wbg_ref)
    streams = [dict(rows=slice(i * rows_per_stream, (i + 1) * rows_per_stream)) for i in range(n_streams)]

    def mixer_input(t):
        t['x'] = x_ref[0, t['rows'], :]
        t['h1'] = (_rms(t['x']) * pre1_ref[...] * (1.0 + m[1]) + m[0]).astype(BF16)

    def gate_proj(t):
        t['out_gate'] = [_silu(_dot(t['h1'], gates[br][...])) for br in range(2)]
        t['merge_gate'] = _sigmoid(_dot(t['h1'], wgm_ref[...]))

    def branch_proj(t):
        o = oi_ref[0, t['rows'], :].astype(F32) + os_ref[0, t['rows'], :].astype(F32)
        t['ys'] = []
        for br in range(2):
            parts = []
            for hd in range(HEADS):
                c0 = br * BRANCH_W + hd * HEAD_DIM
                parts.append(_rms(o[:, c0:c0 + HEAD_DIM]) * norms[br][...])
            og = (jnp.concatenate(parts, axis=-1) * t['out_gate'][br]).astype(BF16)
            t['ys'].append(_dot(og, branch_w[br][...]))

    def out_proj(t):
        gm = t['merge_gate']
        merged = gm[:, :d_model] * t['ys'][0] + gm[:, d_model:] * t['ys'][1]
        t['y'] = _dot(merged.astype(BF16), wout_ref[...])

    def ffn_input(t):
        t['z1'] = t['x'] + _rms(t['y']) * post1_ref[...] * m[2]
        t['h2'] = (_rms(t['z1']) * pre2_ref[...] * (1.0 + m[4]) + m[3]).astype(BF16)

    def ffn_up(t):
        g = _dot(t['h2'], wg_ref[...])
        u = _dot(t['h2'], wu_ref[...])
        t['a'] = (_silu(g) * u).astype(BF16)

    def ffn_down(t):
        t['y2'] = _dot(t['a'], wd_ref[...])

    def residual_out(t):
        out_ref[0, t['rows'], :] = t['z1'] + _rms(t['y2']) * post2_ref[...] * m[5]

    for stage in (mixer_input, gate_proj, branch_proj, out_proj, ffn_input, ffn_up, ffn_down, residual_out):
        for t in streams:
            stage(t)


def _tail(oi, o_scan, x, mod3, *consts, tm):
    b, seq, d = x.shape
    kern = functools.partial(_tail_kernel, d_model=d, n_streams=TAIL_STREAMS)
    return pl.pallas_call(
        kern,
        grid=(b, seq // tm),
        in_specs=[
            pl.BlockSpec((1, tm, DIR_W), lambda i, j: (i, j, 0)),
            pl.BlockSpec((1, tm, DIR_W), lambda i, j: (i, j, 0)),
            pl.BlockSpec((1, tm, d), lambda i, j: (i, j, 0)),
            pl.BlockSpec((1, 1, N_MOD * d), lambda i, j: (i + 1, 0, 0)),
        ] + [_const_spec(c.shape) for c in consts],
        out_specs=pl.BlockSpec((1, tm, d), lambda i, j: (i, j, 0)),
        out_shape=jax.ShapeDtypeStruct((b, seq, d), F32),
        compiler_params=pltpu.CompilerParams(
            dimension_semantics=("arbitrary", "arbitrary"),
            vmem_limit_bytes=VMEM_LIMIT_BYTES),
        name="mixer_out_ffn",
    )(oi, o_scan, x, mod3, *consts)


def kernel(x, c, ctx, c_ctx, w_mod, b_mod, norm_pre1, norm_post1, norm_pre2, norm_post2, w_in, hg_lb,
           hg_onorm, gla_w_gk, gla_b_gk, gla_onorm, w_br_hg, w_br_gla, w_out, w_ff_gate, w_ff_up, w_ff_down):
    b, seq, d = x.shape
    ctx_len = ctx.shape[1]
    assert w_mod.shape[0] == 1 and d == DIR_W
    assert seq % CHUNK == 0 and ctx_len % CHUNK == 0
    tm = ctx_len

    c_all = jnp.zeros((MOD_ROWS, d), F32).at[0].set(c_ctx).at[1:b + 1].set(c)
    mod3 = _modulation(c_all, w_mod.reshape(d, N_MOD * d), b_mod).reshape(MOD_ROWS, 1, N_MOD * d)

    def bf(a):
        return a.reshape(a.shape[1:]).astype(BF16)

    w = bf(w_in)
    lr_w = 2 * GLA_RANK
    wgk = jnp.zeros((LR_PAD, 2 * BRANCH_W), F32)
    wgk = wgk.at[:GLA_RANK, :BRANCH_W].set(gla_w_gk[0, 0]).at[GLA_RANK:lr_w, BRANCH_W:].set(gla_w_gk[0, 1])
    bgk = gla_b_gk.reshape(1, 2 * BRANCH_W)

    tail_weights = (w_br_hg, w_br_gla, w_out, w_ff_gate, w_ff_up, w_ff_down)

    def in_projection(exact):
        return _in_projection(x, ctx, mod3, norm_pre1, w, hg_lb, wgk.astype(BF16), bgk,
                              () if exact else tail_weights, tm=tm, exact=exact)

    *fast, excursion, wbh, wbg, wo, wfg, wfu, wfd = in_projection(False)
    tail_bf16 = [cw.reshape(tw.shape[1:]) for cw, tw in zip((wbh, wbg, wo, wfg, wfu, wfd), tail_weights)]

    def rest(qd, kd, v, el, oi):
        o_scan = _state_scan(qd, kd, v, el, ctx_len=ctx_len, seq=seq, group=tm // CHUNK)
        return _tail(oi, o_scan, x, mod3, hg_onorm, gla_onorm, norm_pre1, norm_post1, norm_pre2, norm_post2,
                     w[:, G_HGATE * BRANCH_W:(G_HGATE + 1) * BRANCH_W],
                     w[:, G_GGATE * BRANCH_W:(G_GGATE + 1) * BRANCH_W],
                     w[:, LR_COL + lr_w:],
                     *tail_bf16, tm=TAIL_ROWS)

    return lax.cond(jnp.max(excursion) > EXCURSION_LIMIT_LOG2,
                    lambda: rest(*in_projection(True)[:5]),
                    lambda: rest(*fast))
```

```python
import functools

import jax
import jax.numpy as jnp
from jax import lax
from jax.experimental import pallas as pl
from jax.experimental.pallas import tpu as pltpu

F32 = jnp.float32
BF16 = jnp.bfloat16

EPS = 1e-6
LOG2E = 1.4426950408889634
CHUNK = 64
N_MOD = 6
HEADS = 4
HEAD_DIM = 128
BRANCH_W = HEADS * HEAD_DIM
DIR_W = 2 * BRANCH_W
GLA_RANK = 16
GLA_GATE_NORM = 16.0
LR_PAD = 128
G_HQ, G_HI, G_HF, G_HB, G_HGATE, G_GQ, G_GK, G_GV, G_GGATE = range(9)
LR_COL = 9 * BRANCH_W
MOD_ROWS = 16
VMEM_LIMIT_BYTES = 56 * 1024 * 1024
TAIL_ROWS = 512
TAIL_STREAMS = 2
CAST_ROW_TILE = 16
EXCURSION_LIMIT_LOG2 = 96.0


def _dot(a, b):
    return jnp.dot(a, b, preferred_element_type=F32)


def _dot_nt(a, b):
    return lax.dot_general(a, b, (((1,), (1,)), ((), ())), preferred_element_type=F32)


def _dot_tn(a, b):
    return lax.dot_general(a, b, (((0,), (0,)), ((), ())), preferred_element_type=F32)


def _sigmoid(x):
    return 1.0 / (1.0 + jnp.exp(-x))


def _silu(x):
    return x * _sigmoid(x)


def _rms(x):
    return x * lax.rsqrt(jnp.mean(x * x, axis=-1, keepdims=True) + EPS)


def _const_spec(shape):
    nd = len(shape)
    return pl.BlockSpec(shape, lambda *_: (0,) * nd, pipeline_mode=pl.Buffered(1))


def _mod_kernel(c_ref, w_ref, b_ref, o_ref):
    a = _silu(c_ref[...]).astype(BF16)
    o_ref[...] = _dot(a, w_ref[...].astype(BF16)) + b_ref[...]


def _modulation(c_all, w_mod, b_mod):
    d = c_all.shape[1]
    n = w_mod.shape[1]
    blk = 1024
    return pl.pallas_call(
        _mod_kernel,
        grid=(n // blk,),
        in_specs=[pl.BlockSpec((MOD_ROWS, d), lambda j: (0, 0)),
                  pl.BlockSpec((d, blk), lambda j: (0, j)),
                  pl.BlockSpec((1, blk), lambda j: (0, j))],
        out_specs=pl.BlockSpec((MOD_ROWS, blk), lambda j: (0, j)),
        out_shape=jax.ShapeDtypeStruct((MOD_ROWS, n), F32),
        name="modulation",
    )(c_all, w_mod, b_mod)


def _inproj_kernel(x_ref, ctx_ref, mod_ref, pre_ref, w_ref, lb_ref, wgk_ref, bgk_ref, *rest,
                   d_model, tm, exact, n_cast):
    cast_in, rest = rest[:n_cast], rest[n_cast:]
    qd_ref, kd_ref, v_ref, el_ref, oi_ref, rng_ref = rest[:6]
    cast_out, scratch = rest[6:6 + n_cast], rest[6 + n_cast:]
    n_ch = tm // CHUNK
    mids = (CHUNK // 2, CHUNK // 2 - 1)
    lasts = (CHUNK - 1, 0)

    def body(z_ref, latent):
        if latent:
            for wi_ref, wo_ref in zip(cast_in, cast_out):
                wo_ref[...] = wi_ref[...].astype(BF16)
        mod = mod_ref[0]
        shift, scale = mod[:, :d_model], mod[:, d_model:2 * d_model]
        h = (_rms(z_ref[0]) * pre_ref[...] * (1.0 + scale) + shift).astype(BF16)

        lbr = lb_ref[...]
        e = jnp.exp(lbr - jnp.max(lbr, axis=0, keepdims=True))
        lb = e[0] / jnp.sum(e, axis=0)

        row = lax.broadcasted_iota(jnp.int32, (tm, tm), 0)
        col = lax.broadcasted_iota(jnp.int32, (tm, tm), 1)
        shift_bits = CHUNK.bit_length() - 1
        same = jnp.right_shift(row, shift_bits) == jnp.right_shift(col, shift_bits)
        masks = (same & (col <= row), same & (col >= row))
        tris = tuple(jnp.where(m, 1.0, 0.0).astype(BF16) for m in masks)

        def proj(g):
            return _dot(h, w_ref[:, g * BRANCH_W:(g + 1) * BRANCH_W])

        span = []

        def prep(q, k, lf, d, col0):
            lf2 = lf * LOG2E
            hi = lf2.astype(BF16)
            md = (lf2 - hi.astype(F32)).astype(BF16)
            c = _dot(tris[d], hi) + _dot(tris[d], md)
            qts, kts = [], []
            for n in range(n_ch):
                rows = slice(n * CHUNK, (n + 1) * CHUNK)
                cj = c[rows]
                cl = cj[lasts[d]:lasts[d] + 1]
                el_ref[0, n, :, col0:col0 + BRANCH_W] = jnp.exp2(cl)
                if exact or not latent:
                    kd_ref[0, rows, col0:col0 + BRANCH_W] = (k[rows] * jnp.exp2(cl - cj)).astype(BF16)
                    if latent:
                        qd_ref[0, rows, col0:col0 + BRANCH_W] = (q[rows] * jnp.exp2(cj)).astype(BF16)
                    continue
                r = cj[mids[d]:mids[d] + 1]
                dq = cj - r
                span.append(jnp.maximum(jnp.abs(dq[0:1]), jnp.abs(dq[CHUNK - 1:CHUNK])))
                qt = q[rows] * jnp.exp2(dq)
                kt = k[rows] * jnp.exp2(-dq)
                qd_ref[0, rows, col0:col0 + BRANCH_W] = (qt * jnp.exp2(r)).astype(BF16)
                kd_ref[0, rows, col0:col0 + BRANCH_W] = (kt * jnp.exp2(cl - r)).astype(BF16)
                qts.append(qt.astype(BF16))
                kts.append(kt.astype(BF16))
            if not latent:
                return None
            if exact:
                return q, k, c
            return jnp.concatenate(qts, axis=0), jnp.concatenate(kts, axis=0)

        def intra_exact(qkc, v, d):
            q, k, c = qkc
            k_s, v_s, c_s = scratch
            k_s[...] = k
            v_s[...] = v.astype(F32)
            c_s[...] = c
            rowid = lax.broadcasted_iota(jnp.int32, (CHUNK, 1), 0)
            outs = []
            for n in range(n_ch):
                base = n * CHUNK
                qj = q[base:base + CHUNK]
                cj = c[base:base + CHUNK]

                def step(s, acc):
                    cs = c_s[pl.ds(base + s, 1), :]
                    ks = k_s[pl.ds(base + s, 1), :]
                    vs = v_s[pl.ds(base + s, 1), :]
                    at_or_after = (rowid >= s) if d == 0 else (rowid <= s)
                    p = jnp.where(at_or_after, qj * jnp.exp2(jnp.minimum(cj - cs, 0.0)) * ks, 0.0)
                    parts = []
                    for hd in range(HEADS):
                        sl = slice(hd * HEAD_DIM, (hd + 1) * HEAD_DIM)
                        parts.append(jnp.sum(p[:, sl], axis=-1, keepdims=True) * vs[:, sl])
                    return acc + jnp.concatenate(parts, axis=-1)

                outs.append(lax.fori_loop(0, CHUNK, step, jnp.zeros((CHUNK, BRANCH_W), F32)))
            return jnp.concatenate(outs, axis=0)

        def intra(qk, v, d):
            if not latent:
                return None
            if exact:
                return intra_exact(qk, v, d)
            qt, kt = qk
            outs = []
            for hd in range(HEADS):
                sl = slice(hd * HEAD_DIM, (hd + 1) * HEAD_DIM)
                a = _dot_nt(qt[:, sl], kt[:, sl])
                a = jnp.where(masks[d], a, 0.0).astype(BF16)
                outs.append(_dot(a, v[:, sl]))
            return jnp.concatenate(outs, axis=-1)

        def hg_gate(raw, d):
            lbd = lb[d:d + 1]
            f = lbd + (1.0 - lbd) * _sigmoid(raw)
            return 1.0 - f, jnp.log(f)

        pq_h = proj(G_HQ) if latent else None
        pv_h = proj(G_HI)
        pf = [proj(G_HF + d) for d in range(2)]
        q_h = _silu(pq_h) if latent else None
        v_h = pv_h.astype(BF16)
        v_ref[0, :, 0:BRANCH_W] = v_h
        qk_h0 = prep(q_h, *hg_gate(pf[0], 0), 0, 0)
        pq_g = proj(G_GQ) if latent else None
        pk_g = proj(G_GK)
        pv_g = proj(G_GV)
        lr = _dot(h, w_ref[:, LR_COL:LR_COL + LR_PAD]).astype(BF16)
        xg = _dot(lr, wgk_ref[...]) + bgk_ref[...]
        qk_h1 = prep(q_h, *hg_gate(pf[1], 1), 1, DIR_W)
        o0 = intra(qk_h0, v_h, 0)
        o1 = intra(qk_h1, v_h, 1)
        q_g = pq_g * HEAD_DIM ** -0.5 if latent else None
        v_g = pv_g.astype(BF16)
        v_ref[0, :, BRANCH_W:2 * BRANCH_W] = v_g
        ls = (jnp.minimum(xg, 0.0) - jnp.log(1.0 + jnp.exp(-jnp.abs(xg)))) * (1.0 / GLA_GATE_NORM)
        qk_g0 = prep(q_g, pk_g, ls[:, :BRANCH_W], 0, BRANCH_W)
        if latent:
            oi_ref[0, :, 0:BRANCH_W] = (o0 + o1).astype(BF16)
        qk_g1 = prep(q_g, pk_g, ls[:, BRANCH_W:], 1, DIR_W + BRANCH_W)
        o0 = intra(qk_g0, v_g, 0)
        o1 = intra(qk_g1, v_g, 1)
        if latent:
            oi_ref[0, :, BRANCH_W:2 * BRANCH_W] = (o0 + o1).astype(BF16)

        widest = jnp.zeros((1, BRANCH_W), F32)
        for piece in span:
            widest = jnp.maximum(widest, piece)
        lanes = widest[:, 0:HEAD_DIM]
        for hd in range(1, HEADS):
            lanes = jnp.maximum(lanes, widest[:, hd * HEAD_DIM:(hd + 1) * HEAD_DIM])
        rng_ref[0, 0] = lanes

    j = pl.program_id(1)

    @pl.when(j == 0)
    def _():
        body(ctx_ref, False)

    @pl.when(j > 0)
    def _():
        body(x_ref, True)


def _in_projection(x, ctx, mod3, norm_pre, w_in_r, hg_lb, wgk, bgk, cast_weights, *, tm, exact):
    b, seq, d = x.shape
    ctx_len = ctx.shape[1]
    assert ctx_len == tm and tm % CHUNK == 0
    t_all = ctx_len + seq
    n_lat = seq // tm
    n_ch = tm // CHUNK
    nw = w_in_r.shape[1]
    n_cast = len(cast_weights)
    cast_2d = [cw.reshape(cw.shape[-2:]) for cw in cast_weights]
    kern = functools.partial(_inproj_kernel, d_model=d, tm=tm, exact=exact, n_cast=n_cast)

    def cast_spec(cw):
        tiles = cw.shape[0] // CAST_ROW_TILE
        n_slabs = max(n for n in range(1, b * n_lat + 1) if tiles % n == 0)
        return pl.BlockSpec(
            (cw.shape[0] // n_slabs, cw.shape[1]),
            lambda i, j: (jnp.minimum(i * n_lat + jnp.maximum(j - 1, 0), n_slabs - 1), 0))

    def all_rows(w):
        return pl.BlockSpec((1, tm, w), lambda i, j: (i, j, 0))

    def latent_rows(w):
        return pl.BlockSpec((1, tm, w), lambda i, j: (i, jnp.maximum(j - 1, 0), 0))

    return pl.pallas_call(
        kern,
        grid=(b, n_lat + 1),
        in_specs=[
            latent_rows(d),
            pl.BlockSpec((1, tm, d), lambda i, j: (i, 0, 0)),
            pl.BlockSpec((1, 1, N_MOD * d), lambda i, j: (jnp.where(j == 0, 0, i + 1), 0, 0)),
            _const_spec((1, d)),
            _const_spec((d, nw)),
            _const_spec(hg_lb.shape),
            _const_spec(wgk.shape),
            _const_spec(bgk.shape),
        ] + [cast_spec(cw) for cw in cast_2d],
        out_specs=[
            latent_rows(2 * DIR_W),
            all_rows(2 * DIR_W),
            all_rows(DIR_W),
            pl.BlockSpec((1, n_ch, 1, 2 * DIR_W), lambda i, j: (i, j, 0, 0)),
            latent_rows(DIR_W),
            pl.BlockSpec((1, 1, 1, HEAD_DIM), lambda i, j: (i, j, 0, 0)),
        ] + [cast_spec(cw) for cw in cast_2d],
        out_shape=[
            jax.ShapeDtypeStruct((b, seq, 2 * DIR_W), BF16),
            jax.ShapeDtypeStruct((b, t_all, 2 * DIR_W), BF16),
            jax.ShapeDtypeStruct((b, t_all, DIR_W), BF16),
            jax.ShapeDtypeStruct((b, t_all // CHUNK, 1, 2 * DIR_W), F32),
            jax.ShapeDtypeStruct((b, seq, DIR_W), BF16),
            jax.ShapeDtypeStruct((b, n_lat + 1, 1, HEAD_DIM), F32),
        ] + [jax.ShapeDtypeStruct(cw.shape, BF16) for cw in cast_2d],
        scratch_shapes=[pltpu.VMEM((tm, BRANCH_W), F32)] * 3 if exact else [],
        compiler_params=pltpu.CompilerParams(
            dimension_semantics=("arbitrary", "arbitrary"),
            vmem_limit_bytes=VMEM_LIMIT_BYTES),
        name="in_projection_exact" if exact else "in_projection",
    )(x, ctx, mod3, norm_pre, w_in_r, hg_lb, wgk, bgk, *cast_2d)


def _scan_kernel(qdf_ref, qdb_ref, kdf_ref, kdb_ref, vf_ref, vb_ref, elf_ref, elb_ref,
                 o_ref, s_ref, acc_ref, *, ctx_steps, n_steps, group):
    s = pl.program_id(1)
    rows_per_step = group * CHUNK

    @pl.when(s == 0)
    def _():
        s_ref[...] = jnp.zeros_like(s_ref)
        acc_ref[...] = jnp.zeros_like(acc_ref)

    def run(with_out):
        if with_out:
            base = (pl.multiple_of((s - ctx_steps) * rows_per_step, rows_per_step),
                    pl.multiple_of((n_steps - 1 - s) * rows_per_step, rows_per_step))
        for g in range(group):
            for d, (qd_ref, kd_ref, v_ref, el_ref) in enumerate((
                    (qdf_ref, kdf_ref, vf_ref, elf_ref), (qdb_ref, kdb_ref, vb_ref, elb_ref))):
                ci = g if d == 0 else group - 1 - g
                rows = slice(ci * CHUNK, (ci + 1) * CHUNK)
                e_last = el_ref[0, ci]
                outs = []
                for u in range(2 * HEADS):
                    sl = slice(u * HEAD_DIM, (u + 1) * HEAD_DIM)
                    st = s_ref[d * 2 * HEADS + u]
                    if with_out:
                        outs.append(_dot_nt(qd_ref[0, rows, sl], st.astype(BF16)))
                    s_ref[d * 2 * HEADS + u] = (st * e_last[:, sl]
                                                + _dot_tn(v_ref[0, rows, sl], kd_ref[0, rows, sl]))
                if with_out:
                    acc_ref[pl.ds(base[d] + ci * CHUNK, CHUNK), :] += jnp.concatenate(outs, axis=-1)

    @pl.when(s < ctx_steps)
    def _():
        run(False)

    @pl.when(s >= ctx_steps)
    def _():
        run(True)

    @pl.when(s == n_steps - 1)
    def _():
        o_ref[0] = acc_ref[...].astype(o_ref.dtype)


def _state_scan(qd, kd, v, el, *, ctx_len, seq, group):
    b = qd.shape[0]
    rows_per_step = group * CHUNK
    assert ctx_len % rows_per_step == 0 and seq % rows_per_step == 0
    ctx_steps = ctx_len // rows_per_step
    n_steps = (ctx_len + seq) // rows_per_step

    def bw_block(s):
        return jnp.where(s < ctx_steps, ctx_steps - 1 - s, n_steps + ctx_steps - 1 - s)

    def fw_rows(i, s):
        return (i, s, 0)

    def bw_rows(i, s):
        return (i, bw_block(s), 1)

    def bw_rows_v(i, s):
        return (i, bw_block(s), 0)

    def fw_lat(i, s):
        return (i, jnp.maximum(s, ctx_steps) - ctx_steps, 0)

    def bw_lat(i, s):
        return (i, n_steps - 1 - jnp.maximum(s, ctx_steps), 1)

    kern = functools.partial(_scan_kernel, ctx_steps=ctx_steps, n_steps=n_steps, group=group)
    blk = (1, rows_per_step, DIR_W)
    eblk = (1, group, 1, DIR_W)
    return pl.pallas_call(
        kern,
        grid=(b, n_steps),
        in_specs=[pl.BlockSpec(blk, fw_lat), pl.BlockSpec(blk, bw_lat),
                  pl.BlockSpec(blk, fw_rows), pl.BlockSpec(blk, bw_rows),
                  pl.BlockSpec(blk, fw_rows), pl.BlockSpec(blk, bw_rows_v),
                  pl.BlockSpec(eblk, lambda i, s: (i, s, 0, 0)),
                  pl.BlockSpec(eblk, lambda i, s: (i, bw_block(s), 0, 1))],
        out_specs=pl.BlockSpec((1, seq, DIR_W), lambda i, s: (i, 0, 0)),
        out_shape=jax.ShapeDtypeStruct((b, seq, DIR_W), BF16),
        scratch_shapes=[pltpu.VMEM((4 * HEADS, HEAD_DIM, HEAD_DIM), F32),
                        pltpu.VMEM((seq, DIR_W), F32)],
        compiler_params=pltpu.CompilerParams(
            dimension_semantics=("arbitrary", "arbitrary"),
            vmem_limit_bytes=VMEM_LIMIT_BYTES),
        name="state_scan",
    )(qd, qd, kd, kd, v, v, el, el)


def _tail_kernel(oi_ref, os_ref, x_ref, mod_ref, hgn_ref, gln_ref, pre1_ref, post1_ref, pre2_ref, post2_ref,
                 wgh_ref, wgg_ref, wgm_ref, wbh_ref, wbg_ref, wout_ref, wg_ref, wu_ref, wd_ref, out_ref,
                 *, d_model, n_streams):
    mod = mod_ref[0]
    m = [mod[:, i * d_model:(i + 1) * d_model] for i in range(N_MOD)]
    rows_per_stream = x_ref.shape[1] // n_streams
    gates = (wgh_ref, wgg_ref)
    norms = (hgn_ref, gln_ref)
    branch_w = (wbh_ref, wbg_ref)
    streams = [dict(rows=slice(i * rows_per_stream, (i + 1) * rows_per_stream)) for i in range(n_streams)]

    def mixer_input(t):
        t['x'] = x_ref[0, t['rows'], :]
        t['h1'] = (_rms(t['x']) * pre1_ref[...] * (1.0 + m[1]) + m[0]).astype(BF16)

    def gate_proj(t):
        t['out_gate'] = [_silu(_dot(t['h1'], gates[br][...])) for br in range(2)]
        t['merge_gate'] = _sigmoid(_dot(t['h1'], wgm_ref[...]))

    def branch_proj(t):
        o = oi_ref[0, t['rows'], :].astype(F32) + os_ref[0, t['rows'], :].astype(F32)
        t['ys'] = []
        for br in range(2):
            parts = []
            for hd in range(HEADS):
                c0 = br * BRANCH_W + hd * HEAD_DIM
                parts.append(_rms(o[:, c0:c0 + HEAD_DIM]) * norms[br][...])
            og = (jnp.concatenate(parts, axis=-1) * t['out_gate'][br]).astype(BF16)
            t['ys'].append(_dot(og, branch_w[br][...]))

    def out_proj(t):
        gm = t['merge_gate']
        merged = gm[:, :d_model] * t['ys'][0] + gm[:, d_model:] * t['ys'][1]
        t['y'] = _dot(merged.astype(BF16), wout_ref[...])

    def ffn_input(t):
        t['z1'] = t['x'] + _rms(t['y']) * post1_ref[...] * m[2]
        t['h2'] = (_rms(t['z1']) * pre2_ref[...] * (1.0 + m[4]) + m[3]).astype(BF16)

    def ffn_up(t):
        g = _dot(t['h2'], wg_ref[...])
        u = _dot(t['h2'], wu_ref[...])
        t['a'] = (_silu(g) * u).astype(BF16)

    def ffn_down(t):
        t['y2'] = _dot(t['a'], wd_ref[...])

    def residual_out(t):
        out_ref[0, t['rows'], :] = t['z1'] + _rms(t['y2']) * post2_ref[...] * m[5]

    for stage in (mixer_input, gate_proj, branch_proj, out_proj, ffn_input, ffn_up, ffn_down, residual_out):
        for t in streams:
            stage(t)


def _tail(oi, o_scan, x, mod3, *consts, tm):
    b, seq, d = x.shape
    kern = functools.partial(_tail_kernel, d_model=d, n_streams=TAIL_STREAMS)
    return pl.pallas_call(
        kern,
        grid=(b, seq // tm),
        in_specs=[
            pl.BlockSpec((1, tm, DIR_W), lambda i, j: (i, j, 0)),
            pl.BlockSpec((1, tm, DIR_W), lambda i, j: (i, j, 0)),
            pl.BlockSpec((1, tm, d), lambda i, j: (i, j, 0)),
            pl.BlockSpec((1, 1, N_MOD * d), lambda i, j: (i + 1, 0, 0)),
        ] + [_const_spec(c.shape) for c in consts],
        out_specs=pl.BlockSpec((1, tm, d), lambda i, j: (i, j, 0)),
        out_shape=jax.ShapeDtypeStruct((b, seq, d), F32),
        compiler_params=pltpu.CompilerParams(
            dimension_semantics=("arbitrary", "arbitrary"),
            vmem_limit_bytes=VMEM_LIMIT_BYTES),
        name="mixer_out_ffn",
    )(oi, o_scan, x, mod3, *consts)


def kernel(x, c, ctx, c_ctx, w_mod, b_mod, norm_pre1, norm_post1, norm_pre2, norm_post2, w_in, hg_lb,
           hg_onorm, gla_w_gk, gla_b_gk, gla_onorm, w_br_hg, w_br_gla, w_out, w_ff_gate, w_ff_up, w_ff_down):
    b, seq, d = x.shape
    ctx_len = ctx.shape[1]
    assert w_mod.shape[0] == 1 and d == DIR_W
    assert seq % CHUNK == 0 and ctx_len % CHUNK == 0
    tm = ctx_len

    c_all = jnp.zeros((MOD_ROWS, d), F32).at[0].set(c_ctx).at[1:b + 1].set(c)
    mod3 = _modulation(c_all, w_mod.reshape(d, N_MOD * d), b_mod).reshape(MOD_ROWS, 1, N_MOD * d)

    def bf(a):
        return a.reshape(a.shape[1:]).astype(BF16)

    w = bf(w_in)
    lr_w = 2 * GLA_RANK
    wgk = jnp.zeros((LR_PAD, 2 * BRANCH_W), F32)
    wgk = wgk.at[:GLA_RANK, :BRANCH_W].set(gla_w_gk[0, 0]).at[GLA_RANK:lr_w, BRANCH_W:].set(gla_w_gk[0, 1])
    bgk = gla_b_gk.reshape(1, 2 * BRANCH_W)

    tail_weights = (w_br_hg, w_br_gla, w_out, w_ff_gate, w_ff_up, w_ff_down)

    def in_projection(exact):
        return _in_projection(x, ctx, mod3, norm_pre1, w, hg_lb, wgk.astype(BF16), bgk,
                              () if exact else tail_weights, tm=tm, exact=exact)

    *fast, excursion, wbh, wbg, wo, wfg, wfu, wfd = in_projection(False)
    tail_bf16 = (wbh, wbg, wo, wfg, wfu, wfd)

    def rest(qd, kd, v, el, oi):
        o_scan = _state_scan(qd, kd, v, el, ctx_len=ctx_len, seq=seq, group=tm // CHUNK)
        return _tail(oi, o_scan, x, mod3, hg_onorm, gla_onorm, norm_pre1, norm_post1, norm_pre2, norm_post2,
                     w[:, G_HGATE * BRANCH_W:(G_HGATE + 1) * BRANCH_W],
                     w[:, G_GGATE * BRANCH_W:(G_GGATE + 1) * BRANCH_W],
                     w[:, LR_COL + lr_w:],
                     *tail_bf16, tm=TAIL_ROWS)

    return lax.cond(jnp.max(excursion) > EXCURSION_LIMIT_LOG2,
                    lambda: rest(*in_projection(True)[:5]),
                    lambda: rest(*fast))
```

```python
import functools

import jax
import jax.numpy as jnp
from jax import lax
from jax.experimental import pallas as pl
from jax.experimental.pallas import tpu as pltpu

F32 = jnp.float32
BF16 = jnp.bfloat16

EPS = 1e-6
LOG2E = 1.4426950408889634
CHUNK = 64
N_MOD = 6
HEADS = 4
HEAD_DIM = 128
BRANCH_W = HEADS * HEAD_DIM
DIR_W = 2 * BRANCH_W
GLA_RANK = 16
GLA_GATE_NORM = 16.0
LR_PAD = 128
G_HQ, G_HI, G_HF, G_HB, G_HGATE, G_GQ, G_GK, G_GV, G_GGATE = range(9)
LR_COL = 9 * BRANCH_W
MOD_ROWS = 16
VMEM_LIMIT_BYTES = 56 * 1024 * 1024
TAIL_ROWS = 512
TAIL_STREAMS = 2
CAST_ROW_TILE = 16
EXCURSION_LIMIT_LOG2 = 96.0


def _dot(a, b):
    return jnp.dot(a, b, preferred_element_type=F32)


def _dot_nt(a, b):
    return lax.dot_general(a, b, (((1,), (1,)), ((), ())), preferred_element_type=F32)


def _dot_tn(a, b):
    return lax.dot_general(a, b, (((0,), (0,)), ((), ())), preferred_element_type=F32)


def _sigmoid(x):
    return 1.0 / (1.0 + jnp.exp(-x))


def _silu(x):
    return x * _sigmoid(x)


def _rms(x):
    return x * lax.rsqrt(jnp.mean(x * x, axis=-1, keepdims=True) + EPS)


def _const_spec(shape):
    nd = len(shape)
    return pl.BlockSpec(shape, lambda *_: (0,) * nd, pipeline_mode=pl.Buffered(1))


def _mod_kernel(c_ref, w_ref, b_ref, o_ref):
    a = _silu(c_ref[...]).astype(BF16)
    o_ref[...] = _dot(a, w_ref[...].astype(BF16)) + b_ref[...]


def _modulation(c_all, w_mod, b_mod):
    d = c_all.shape[1]
    n = w_mod.shape[1]
    blk = 1024
    return pl.pallas_call(
        _mod_kernel,
        grid=(n // blk,),
        in_specs=[pl.BlockSpec((MOD_ROWS, d), lambda j: (0, 0)),
                  pl.BlockSpec((d, blk), lambda j: (0, j)),
                  pl.BlockSpec((1, blk), lambda j: (0, j))],
        out_specs=pl.BlockSpec((MOD_ROWS, blk), lambda j: (0, j)),
        out_shape=jax.ShapeDtypeStruct((MOD_ROWS, n), F32),
        name="modulation",
    )(c_all, w_mod, b_mod)


def _state_step(st, e_last, v, kd):
    return st * e_last + _dot_tn(v, kd)


def _inproj_kernel(x_ref, ctx_ref, mod_ref, pre_ref, w_ref, lb_ref, wgk_ref, bgk_ref, *rest,
                   d_model, tm, exact, n_cast):
    cast_in, rest = rest[:n_cast], rest[n_cast:]
    qdb_ref, kdb_ref, v_ref, elb_ref, oi_ref, rng_ref, sbw0_ref = rest[:7]
    cast_out, rest = rest[7:7 + n_cast], rest[7 + n_cast:]
    sfw_ref, scratch = rest[0], rest[1:]
    n_ch = tm // CHUNK
    mids = (CHUNK // 2, CHUNK // 2 - 1)
    lasts = (CHUNK - 1, 0)

    def body(z_ref, latent):
        if latent:
            for wi_ref, wo_ref in zip(cast_in, cast_out):
                wo_ref[...] = wi_ref[...].astype(BF16)
        else:
            sfw_ref[...] = jnp.zeros_like(sfw_ref)
        mod = mod_ref[0]
        shift, scale = mod[:, :d_model], mod[:, d_model:2 * d_model]
        h = (_rms(z_ref[0]) * pre_ref[...] * (1.0 + scale) + shift).astype(BF16)

        lbr = lb_ref[...]
        e = jnp.exp(lbr - jnp.max(lbr, axis=0, keepdims=True))
        lb = e[0] / jnp.sum(e, axis=0)

        row = lax.broadcasted_iota(jnp.int32, (tm, tm), 0)
        col = lax.broadcasted_iota(jnp.int32, (tm, tm), 1)
        shift_bits = CHUNK.bit_length() - 1
        same = jnp.right_shift(row, shift_bits) == jnp.right_shift(col, shift_bits)
        masks = (same & (col <= row), same & (col >= row))
        tris = tuple(jnp.where(m, 1.0, 0.0).astype(BF16) for m in masks)

        def proj(g):
            return _dot(h, w_ref[:, g * BRANCH_W:(g + 1) * BRANCH_W])

        span = []

        def prep(q, k, lf, d):
            lf2 = lf * LOG2E
            hi = lf2.astype(BF16)
            md = (lf2 - hi.astype(F32)).astype(BF16)
            c = _dot(tris[d], hi) + _dot(tris[d], md)
            qts, kts, qds, kds, els = [], [], [], [], []
            for n in range(n_ch):
                rows = slice(n * CHUNK, (n + 1) * CHUNK)
                cj = c[rows]
                cl = cj[lasts[d]:lasts[d] + 1]
                els.append(jnp.exp2(cl))
                if exact or not latent:
                    kds.append((k[rows] * jnp.exp2(cl - cj)).astype(BF16))
                    if latent:
                        qds.append((q[rows] * jnp.exp2(cj)).astype(BF16))
                    continue
                r = cj[mids[d]:mids[d] + 1]
                dq = cj - r
                span.append(jnp.maximum(jnp.abs(dq[0:1]), jnp.abs(dq[CHUNK - 1:CHUNK])))
                qt = q[rows] * jnp.exp2(dq)
                kt = k[rows] * jnp.exp2(-dq)
                qds.append((qt * jnp.exp2(r)).astype(BF16))
                kds.append((kt * jnp.exp2(cl - r)).astype(BF16))
                qts.append(qt.astype(BF16))
                kts.append(kt.astype(BF16))
            scores = None
            if latent:
                scores = (q, k, c) if exact else (jnp.concatenate(qts, axis=0), jnp.concatenate(kts, axis=0))
            return dict(qd=qds, kd=kds, el=els, scores=scores)

        def intra_exact(qkc, v, d):
            q, k, c = qkc
            k_s, v_s, c_s = scratch
            k_s[...] = k
            v_s[...] = v.astype(F32)
            c_s[...] = c
            rowid = lax.broadcasted_iota(jnp.int32, (CHUNK, 1), 0)
            outs = []
            for n in range(n_ch):
                base = n * CHUNK
                qj = q[base:base + CHUNK]
                cj = c[base:base + CHUNK]

                def step(s, acc):
                    cs = c_s[pl.ds(base + s, 1), :]
                    ks = k_s[pl.ds(base + s, 1), :]
                    vs = v_s[pl.ds(base + s, 1), :]
                    at_or_after = (rowid >= s) if d == 0 else (rowid <= s)
                    p = jnp.where(at_or_after, qj * jnp.exp2(jnp.minimum(cj - cs, 0.0)) * ks, 0.0)
                    parts = []
                    for hd in range(HEADS):
                        sl = slice(hd * HEAD_DIM, (hd + 1) * HEAD_DIM)
                        parts.append(jnp.sum(p[:, sl], axis=-1, keepdims=True) * vs[:, sl])
                    return acc + jnp.concatenate(parts, axis=-1)

                outs.append(lax.fori_loop(0, CHUNK, step, jnp.zeros((CHUNK, BRANCH_W), F32)))
            return jnp.concatenate(outs, axis=0)

        def intra(p, v, d):
            if not latent:
                return None
            if exact:
                return intra_exact(p['scores'], v, d)
            qt, kt = p['scores']
            outs = []
            for hd in range(HEADS):
                sl = slice(hd * HEAD_DIM, (hd + 1) * HEAD_DIM)
                a = _dot_nt(qt[:, sl], kt[:, sl])
                a = jnp.where(masks[d], a, 0.0).astype(BF16)
                outs.append(_dot(a, v[:, sl]))
            return jnp.concatenate(outs, axis=-1)

        def scan_fw(p, v, branch):
            outs = []
            for n in range(n_ch):
                rows = slice(n * CHUNK, (n + 1) * CHUNK)
                parts = []
                for hd in range(HEADS):
                    sl = slice(hd * HEAD_DIM, (hd + 1) * HEAD_DIM)
                    st = sfw_ref[branch * HEADS + hd]
                    if latent:
                        parts.append(_dot_nt(p['qd'][n][:, sl], st.astype(BF16)))
                    sfw_ref[branch * HEADS + hd] = _state_step(st, p['el'][n][:, sl], v[rows, sl], p['kd'][n][:, sl])
                if latent:
                    outs.append(jnp.concatenate(parts, axis=-1))
            return jnp.concatenate(outs, axis=0) if latent else None

        def emit_bw(p, v, branch):
            c0 = branch * BRANCH_W
            if latent:
                for n in range(n_ch):
                    rows = slice(n * CHUNK, (n + 1) * CHUNK)
                    qdb_ref[0, rows, c0:c0 + BRANCH_W] = p['qd'][n]
                    kdb_ref[0, rows, c0:c0 + BRANCH_W] = p['kd'][n]
                    elb_ref[0, n, :, c0:c0 + BRANCH_W] = p['el'][n]
                return
            for hd in range(HEADS):
                sl = slice(hd * HEAD_DIM, (hd + 1) * HEAD_DIM)
                st = jnp.zeros((HEAD_DIM, HEAD_DIM), F32)
                for n in reversed(range(n_ch)):
                    rows = slice(n * CHUNK, (n + 1) * CHUNK)
                    st = _state_step(st, p['el'][n][:, sl], v[rows, sl], p['kd'][n][:, sl])
                sbw0_ref[0, branch * HEADS + hd] = st

        def hg_gate(raw, d):
            lbd = lb[d:d + 1]
            f = lbd + (1.0 - lbd) * _sigmoid(raw)
            return 1.0 - f, jnp.log(f)

        def total(*parts):
            return functools.reduce(lambda a, b: a + b, parts).astype(BF16)

        pq_h = proj(G_HQ) if latent else None
        pv_h = proj(G_HI)
        pf = [proj(G_HF + d) for d in range(2)]
        q_h = _silu(pq_h) if latent else None
        v_h = pv_h.astype(BF16)
        p_h0 = prep(q_h, *hg_gate(pf[0], 0), 0)
        pq_g = proj(G_GQ) if latent else None
        pk_g = proj(G_GK)
        pv_g = proj(G_GV)
        lr = _dot(h, w_ref[:, LR_COL:LR_COL + LR_PAD]).astype(BF16)
        xg = _dot(lr, wgk_ref[...]) + bgk_ref[...]
        p_h1 = prep(q_h, *hg_gate(pf[1], 1), 1)
        emit_bw(p_h1, v_h, 0)
        o0 = intra(p_h0, v_h, 0)
        os0 = scan_fw(p_h0, v_h, 0)
        q_g = pq_g * HEAD_DIM ** -0.5 if latent else None
        v_g = pv_g.astype(BF16)
        if latent:
            v_ref[0, :, 0:BRANCH_W] = v_h
            v_ref[0, :, BRANCH_W:2 * BRANCH_W] = v_g
        ls = (jnp.minimum(xg, 0.0) - jnp.log(1.0 + jnp.exp(-jnp.abs(xg)))) * (1.0 / GLA_GATE_NORM)
        p_g0 = prep(q_g, pk_g, ls[:, :BRANCH_W], 0)
        o1 = intra(p_h1, v_h, 1)
        if latent:
            oi_ref[0, :, 0:BRANCH_W] = total(o0, o1, os0)
        p_g1 = prep(q_g, pk_g, ls[:, BRANCH_W:], 1)
        emit_bw(p_g1, v_g, 1)
        o0 = intra(p_g0, v_g, 0)
        os0 = scan_fw(p_g0, v_g, 1)
        o1 = intra(p_g1, v_g, 1)
        if latent:
            oi_ref[0, :, BRANCH_W:2 * BRANCH_W] = total(o0, o1, os0)

        widest = jnp.zeros((1, BRANCH_W), F32)
        for piece in span:
            widest = jnp.maximum(widest, piece)
        lanes = widest[:, 0:HEAD_DIM]
        for hd in range(1, HEADS):
            lanes = jnp.maximum(lanes, widest[:, hd * HEAD_DIM:(hd + 1) * HEAD_DIM])
        rng_ref[0, 0] = lanes

    j = pl.program_id(1)

    @pl.when(j == 0)
    def _():
        body(ctx_ref, False)

    @pl.when(j > 0)
    def _():
        body(x_ref, True)


def _in_projection(x, ctx, mod3, norm_pre, w_in_r, hg_lb, wgk, bgk, cast_weights, *, tm, exact):
    b, seq, d = x.shape
    ctx_len = ctx.shape[1]
    assert ctx_len == tm and tm % CHUNK == 0
    n_lat = seq // tm
    n_ch = tm // CHUNK
    nw = w_in_r.shape[1]
    n_cast = len(cast_weights)
    cast_2d = [cw.reshape(cw.shape[-2:]) for cw in cast_weights]
    kern = functools.partial(_inproj_kernel, d_model=d, tm=tm, exact=exact, n_cast=n_cast)

    def cast_spec(cw):
        tiles = cw.shape[0] // CAST_ROW_TILE
        n_slabs = max(n for n in range(1, b * n_lat + 1) if tiles % n == 0)
        return pl.BlockSpec(
            (cw.shape[0] // n_slabs, cw.shape[1]),
            lambda i, j: (jnp.minimum(i * n_lat + jnp.maximum(j - 1, 0), n_slabs - 1), 0))

    def latent_rows(w):
        return pl.BlockSpec((1, tm, w), lambda i, j: (i, jnp.maximum(j - 1, 0), 0))

    n_state = 2 * HEADS
    return pl.pallas_call(
        kern,
        grid=(b, n_lat + 1),
        in_specs=[
            latent_rows(d),
            pl.BlockSpec((1, tm, d), lambda i, j: (i, 0, 0)),
            pl.BlockSpec((1, 1, N_MOD * d), lambda i, j: (jnp.where(j == 0, 0, i + 1), 0, 0)),
            _const_spec((1, d)),
            _const_spec((d, nw)),
            _const_spec(hg_lb.shape),
            _const_spec(wgk.shape),
            _const_spec(bgk.shape),
        ] + [cast_spec(cw) for cw in cast_2d],
        out_specs=[
            latent_rows(DIR_W),
            latent_rows(DIR_W),
            latent_rows(DIR_W),
            pl.BlockSpec((1, n_ch, 1, DIR_W), lambda i, j: (i, jnp.maximum(j - 1, 0), 0, 0)),
            latent_rows(DIR_W),
            pl.BlockSpec((1, 1, 1, HEAD_DIM), lambda i, j: (i, j, 0, 0)),
            pl.BlockSpec((1, n_state, HEAD_DIM, HEAD_DIM), lambda i, j: (i, 0, 0, 0)),
        ] + [cast_spec(cw) for cw in cast_2d],
        out_shape=[
            jax.ShapeDtypeStruct((b, seq, DIR_W), BF16),
            jax.ShapeDtypeStruct((b, seq, DIR_W), BF16),
            jax.ShapeDtypeStruct((b, seq, DIR_W), BF16),
            jax.ShapeDtypeStruct((b, seq // CHUNK, 1, DIR_W), F32),
            jax.ShapeDtypeStruct((b, seq, DIR_W), BF16),
            jax.ShapeDtypeStruct((b, n_lat + 1, 1, HEAD_DIM), F32),
            jax.ShapeDtypeStruct((b, n_state, HEAD_DIM, HEAD_DIM), F32),
        ] + [jax.ShapeDtypeStruct(cw.shape, BF16) for cw in cast_2d],
        scratch_shapes=([pltpu.VMEM((n_state, HEAD_DIM, HEAD_DIM), F32)]
                        + ([pltpu.VMEM((tm, BRANCH_W), F32)] * 3 if exact else [])),
        compiler_params=pltpu.CompilerParams(
            dimension_semantics=("arbitrary", "arbitrary"),
            vmem_limit_bytes=VMEM_LIMIT_BYTES),
        name="in_projection_exact" if exact else "in_projection",
    )(x, ctx, mod3, norm_pre, w_in_r, hg_lb, wgk, bgk, *cast_2d)


def _tail_kernel(oi_ref, qdb_ref, kdb_ref, v_ref, elb_ref, sbw0_ref, x_ref, mod_ref,
                 hgn_ref, gln_ref, pre1_ref, post1_ref, pre2_ref, post2_ref,
                 wgh_ref, wgg_ref, wgm_ref, wbh_ref, wbg_ref, wout_ref, wg_ref, wu_ref, wd_ref,
                 out_ref, sbw_ref, *, d_model, n_streams):
    mod = mod_ref[0]
    m = [mod[:, i * d_model:(i + 1) * d_model] for i in range(N_MOD)]
    rows_per_stream = x_ref.shape[1] // n_streams
    chunks_per_stream = rows_per_stream // CHUNK
    gates = (wgh_ref, wgg_ref)
    norms = (hgn_ref, gln_ref)
    branch_w = (wbh_ref, wbg_ref)
    streams = [dict(first=i * rows_per_stream, rows=slice(i * rows_per_stream, (i + 1) * rows_per_stream))
               for i in range(n_streams)]

    @pl.when(pl.program_id(1) == 0)
    def _():
        sbw_ref[...] = sbw0_ref[0]

    def scan_bw(t):
        parts = [None] * chunks_per_stream
        for n in reversed(range(chunks_per_stream)):
            rows = slice(t['first'] + n * CHUNK, t['first'] + (n + 1) * CHUNK)
            e_last = elb_ref[0, t['first'] // CHUNK + n]
            heads = []
            for u in range(2 * HEADS):
                sl = slice(u * HEAD_DIM, (u + 1) * HEAD_DIM)
                st = sbw_ref[u]
                heads.append(_dot_nt(qdb_ref[0, rows, sl], st.astype(BF16)))
                sbw_ref[u] = _state_step(st, e_last[:, sl], v_ref[0, rows, sl], kdb_ref[0, rows, sl])
            parts[n] = jnp.concatenate(heads, axis=-1)
        t['o_bw'] = jnp.concatenate(parts, axis=0)

    def mixer_input(t):
        t['x'] = x_ref[0, t['rows'], :]
        t['h1'] = (_rms(t['x']) * pre1_ref[...] * (1.0 + m[1]) + m[0]).astype(BF16)

    def gate_proj(t):
        t['out_gate'] = [_silu(_dot(t['h1'], gates[br][...])) for br in range(2)]
        t['merge_gate'] = _sigmoid(_dot(t['h1'], wgm_ref[...]))

    def branch_proj(t):
        o = oi_ref[0, t['rows'], :].astype(F32) + t['o_bw']
        t['ys'] = []
        for br in range(2):
            parts = []
            for hd in range(HEADS):
                c0 = br * BRANCH_W + hd * HEAD_DIM
                parts.append(_rms(o[:, c0:c0 + HEAD_DIM]) * norms[br][...])
            og = (jnp.concatenate(parts, axis=-1) * t['out_gate'][br]).astype(BF16)
            t['ys'].append(_dot(og, branch_w[br][...]))

    def out_proj(t):
        gm = t['merge_gate']
        merged = gm[:, :d_model] * t['ys'][0] + gm[:, d_model:] * t['ys'][1]
        t['y'] = _dot(merged.astype(BF16), wout_ref[...])

    def ffn_input(t):
        t['z1'] = t['x'] + _rms(t['y']) * post1_ref[...] * m[2]
        t['h2'] = (_rms(t['z1']) * pre2_ref[...] * (1.0 + m[4]) + m[3]).astype(BF16)

    def ffn_up(t):
        g = _dot(t['h2'], wg_ref[...])
        u = _dot(t['h2'], wu_ref[...])
        t['a'] = (_silu(g) * u).astype(BF16)

    def ffn_down(t):
        t['y2'] = _dot(t['a'], wd_ref[...])

    def residual_out(t):
        out_ref[0, t['rows'], :] = t['z1'] + _rms(t['y2']) * post2_ref[...] * m[5]

    for t in reversed(streams):
        scan_bw(t)
    for stage in (mixer_input, gate_proj, branch_proj, out_proj, ffn_input, ffn_up, ffn_down, residual_out):
        for t in streams:
            stage(t)


def _tail(oi, qdb, kdb, v, elb, sbw0, x, mod3, *consts, tm):
    b, seq, d = x.shape
    n_tiles = seq // tm
    kern = functools.partial(_tail_kernel, d_model=d, n_streams=TAIL_STREAMS)

    def rows(w):
        return pl.BlockSpec((1, tm, w), lambda i, j: (i, n_tiles - 1 - j, 0))

    return pl.pallas_call(
        kern,
        grid=(b, n_tiles),
        in_specs=[
            rows(DIR_W), rows(DIR_W), rows(DIR_W), rows(DIR_W),
            pl.BlockSpec((1, tm // CHUNK, 1, DIR_W), lambda i, j: (i, n_tiles - 1 - j, 0, 0)),
            pl.BlockSpec((1,) + sbw0.shape[1:], lambda i, j: (i, 0, 0, 0)),
            rows(d),
            pl.BlockSpec((1, 1, N_MOD * d), lambda i, j: (i + 1, 0, 0)),
        ] + [_const_spec(c.shape) for c in consts],
        out_specs=rows(d),
        out_shape=jax.ShapeDtypeStruct((b, seq, d), F32),
        scratch_shapes=[pltpu.VMEM(sbw0.shape[1:], F32)],
        compiler_params=pltpu.CompilerParams(
            dimension_semantics=("arbitrary", "arbitrary"),
            vmem_limit_bytes=VMEM_LIMIT_BYTES),
        name="mixer_out_ffn",
    )(oi, qdb, kdb, v, elb, sbw0, x, mod3, *consts)


def kernel(x, c, ctx, c_ctx, w_mod, b_mod, norm_pre1, norm_post1, norm_pre2, norm_post2, w_in, hg_lb,
           hg_onorm, gla_w_gk, gla_b_gk, gla_onorm, w_br_hg, w_br_gla, w_out, w_ff_gate, w_ff_up, w_ff_down):
    b, seq, d = x.shape
    ctx_len = ctx.shape[1]
    assert w_mod.shape[0] == 1 and d == DIR_W
    assert seq % CHUNK == 0 and ctx_len % CHUNK == 0
    tm = ctx_len

    c_all = jnp.zeros((MOD_ROWS, d), F32).at[0].set(c_ctx).at[1:b + 1].set(c)
    mod3 = _modulation(c_all, w_mod.reshape(d, N_MOD * d), b_mod).reshape(MOD_ROWS, 1, N_MOD * d)

    def bf(a):
        return a.reshape(a.shape[1:]).astype(BF16)

    w = bf(w_in)
    lr_w = 2 * GLA_RANK
    wgk = jnp.zeros((LR_PAD, 2 * BRANCH_W), F32)
    wgk = wgk.at[:GLA_RANK, :BRANCH_W].set(gla_w_gk[0, 0]).at[GLA_RANK:lr_w, BRANCH_W:].set(gla_w_gk[0, 1])
    bgk = gla_b_gk.reshape(1, 2 * BRANCH_W)

    tail_weights = (w_br_hg, w_br_gla, w_out, w_ff_gate, w_ff_up, w_ff_down)

    def in_projection(exact):
        return _in_projection(x, ctx, mod3, norm_pre1, w, hg_lb, wgk.astype(BF16), bgk,
                              () if exact else tail_weights, tm=tm, exact=exact)

    *fast, excursion, sbw0, wbh, wbg, wo, wfg, wfu, wfd = in_projection(False)
    tail_bf16 = (wbh, wbg, wo, wfg, wfu, wfd)

    def rest(qdb, kdb, v, elb, oi, sbw0):
        return _tail(oi, qdb, kdb, v, elb, sbw0, x, mod3,
                     hg_onorm, gla_onorm, norm_pre1, norm_post1, norm_pre2, norm_post2,
                     w[:, G_HGATE * BRANCH_W:(G_HGATE + 1) * BRANCH_W],
                     w[:, G_GGATE * BRANCH_W:(G_GGATE + 1) * BRANCH_W],
                     w[:, LR_COL + lr_w:],
                     *tail_bf16, tm=TAIL_ROWS)

    def redo_exact():
        qdb, kdb, v, elb, oi, _, sbw0_exact = in_projection(True)[:7]
        return rest(qdb, kdb, v, elb, oi, sbw0_exact)

    return lax.cond(jnp.max(excursion) > EXCURSION_LIMIT_LOG2, redo_exact, lambda: rest(*fast, sbw0))
```

```python
import functools

import jax
import jax.numpy as jnp
from jax import lax
from jax.experimental import pallas as pl
from jax.experimental.pallas import tpu as pltpu

F32 = jnp.float32
BF16 = jnp.bfloat16

EPS = 1e-6
LOG2E = 1.4426950408889634
CHUNK = 64
N_MOD = 6
HEADS = 4
HEAD_DIM = 128
BRANCH_W = HEADS * HEAD_DIM
DIR_W = 2 * BRANCH_W
GLA_RANK = 16
GLA_GATE_NORM = 16.0
LR_PAD = 128
G_HQ, G_HI, G_HF, G_HB, G_HGATE, G_GQ, G_GK, G_GV, G_GGATE = range(9)
LR_COL = 9 * BRANCH_W
MOD_ROWS = 16
VMEM_LIMIT_BYTES = 56 * 1024 * 1024
TAIL_ROWS = 512
TAIL_STREAMS = 2
CAST_ROW_TILE = 16
EXCURSION_LIMIT_LOG2 = 96.0


def _dot(a, b):
    return jnp.dot(a, b, preferred_element_type=F32)


def _dot_nt(a, b):
    return lax.dot_general(a, b, (((1,), (1,)), ((), ())), preferred_element_type=F32)


def _dot_tn(a, b):
    return lax.dot_general(a, b, (((0,), (0,)), ((), ())), preferred_element_type=F32)


def _sigmoid(x):
    return 1.0 / (1.0 + jnp.exp(-x))


def _silu(x):
    return x * _sigmoid(x)


def _rms(x):
    return x * lax.rsqrt(jnp.mean(x * x, axis=-1, keepdims=True) + EPS)


def _const_spec(shape):
    nd = len(shape)
    return pl.BlockSpec(shape, lambda *_: (0,) * nd, pipeline_mode=pl.Buffered(1))


def _mod_kernel(c_ref, w_ref, b_ref, o_ref):
    a = _silu(c_ref[...]).astype(BF16)
    o_ref[...] = _dot(a, w_ref[...].astype(BF16)) + b_ref[...]


def _modulation(c_all, w_mod, b_mod):
    d = c_all.shape[1]
    n = w_mod.shape[1]
    blk = 1024
    return pl.pallas_call(
        _mod_kernel,
        grid=(n // blk,),
        in_specs=[pl.BlockSpec((MOD_ROWS, d), lambda j: (0, 0)),
                  pl.BlockSpec((d, blk), lambda j: (0, j)),
                  pl.BlockSpec((1, blk), lambda j: (0, j))],
        out_specs=pl.BlockSpec((MOD_ROWS, blk), lambda j: (0, j)),
        out_shape=jax.ShapeDtypeStruct((MOD_ROWS, n), F32),
        name="modulation",
    )(c_all, w_mod, b_mod)


def _state_step(st, e_last, v, kd):
    return st * e_last + _dot_tn(v, kd)


def _inproj_kernel(x_ref, ctx_ref, mod_ref, pre_ref, w_ref, lb_ref, wgk_ref, bgk_ref, *rest,
                   d_model, tm, exact, n_cast):
    cast_in, rest = rest[:n_cast], rest[n_cast:]
    qdb_ref, kdb_ref, v_ref, elb_ref, oi_ref, rng_ref, sbw0_ref = rest[:7]
    cast_out, rest = rest[7:7 + n_cast], rest[7 + n_cast:]
    sfw_ref, scratch = rest[0], rest[1:]
    n_ch = tm // CHUNK
    mids = (CHUNK // 2, CHUNK // 2 - 1)
    lasts = (CHUNK - 1, 0)

    def body(z_ref, latent):
        if latent:
            for wi_ref, wo_ref in zip(cast_in, cast_out):
                wo_ref[...] = wi_ref[...].astype(BF16)
        else:
            sfw_ref[...] = jnp.zeros_like(sfw_ref)
        mod = mod_ref[0]
        shift, scale = mod[:, :d_model], mod[:, d_model:2 * d_model]
        h = (_rms(z_ref[0]) * pre_ref[...] * (1.0 + scale) + shift).astype(BF16)

        lbr = lb_ref[...]
        e = jnp.exp(lbr - jnp.max(lbr, axis=0, keepdims=True))
        lb = e[0] / jnp.sum(e, axis=0)

        row = lax.broadcasted_iota(jnp.int32, (tm, tm), 0)
        col = lax.broadcasted_iota(jnp.int32, (tm, tm), 1)
        shift_bits = CHUNK.bit_length() - 1
        same = jnp.right_shift(row, shift_bits) == jnp.right_shift(col, shift_bits)
        masks = (same & (col <= row), same & (col >= row))
        tris = tuple(jnp.where(m, 1.0, 0.0).astype(BF16) for m in masks)

        def proj(g):
            return _dot(h, w_ref[:, g * BRANCH_W:(g + 1) * BRANCH_W])

        span = []

        def prep(q, k, lf, d):
            lf2 = lf * LOG2E
            hi = lf2.astype(BF16)
            md = (lf2 - hi.astype(F32)).astype(BF16)
            c = _dot(tris[d], hi) + _dot(tris[d], md)
            qts, kts, qds, kds, els = [], [], [], [], []
            for n in range(n_ch):
                rows = slice(n * CHUNK, (n + 1) * CHUNK)
                cj = c[rows]
                cl = cj[lasts[d]:lasts[d] + 1]
                els.append(jnp.exp2(cl))
                if exact or not latent:
                    kds.append((k[rows] * jnp.exp2(cl - cj)).astype(BF16))
                    if latent:
                        qds.append((q[rows] * jnp.exp2(cj)).astype(BF16))
                    continue
                r = cj[mids[d]:mids[d] + 1]
                dq = cj - r
                span.append(jnp.maximum(jnp.abs(dq[0:1]), jnp.abs(dq[CHUNK - 1:CHUNK])))
                qt = q[rows] * jnp.exp2(dq)
                kt = k[rows] * jnp.exp2(-dq)
                qds.append((qt * jnp.exp2(r)).astype(BF16))
                kds.append((kt * jnp.exp2(cl - r)).astype(BF16))
                qts.append(qt.astype(BF16))
                kts.append(kt.astype(BF16))
            scores = None
            if latent:
                scores = (q, k, c) if exact else (jnp.concatenate(qts, axis=0), jnp.concatenate(kts, axis=0))
            return dict(qd=qds, kd=kds, el=els, scores=scores)

        def intra_exact(qkc, v, d):
            q, k, c = qkc
            k_s, v_s, c_s = scratch
            k_s[...] = k
            v_s[...] = v.astype(F32)
            c_s[...] = c
            rowid = lax.broadcasted_iota(jnp.int32, (CHUNK, 1), 0)
            outs = []
            for n in range(n_ch):
                base = n * CHUNK
                qj = q[base:base + CHUNK]
                cj = c[base:base + CHUNK]

                def step(s, acc):
                    cs = c_s[pl.ds(base + s, 1), :]
                    ks = k_s[pl.ds(base + s, 1), :]
                    vs = v_s[pl.ds(base + s, 1), :]
                    at_or_after = (rowid >= s) if d == 0 else (rowid <= s)
                    p = jnp.where(at_or_after, qj * jnp.exp2(jnp.minimum(cj - cs, 0.0)) * ks, 0.0)
                    parts = []
                    for hd in range(HEADS):
                        sl = slice(hd * HEAD_DIM, (hd + 1) * HEAD_DIM)
                        parts.append(jnp.sum(p[:, sl], axis=-1, keepdims=True) * vs[:, sl])
                    return acc + jnp.concatenate(parts, axis=-1)

                outs.append(lax.fori_loop(0, CHUNK, step, jnp.zeros((CHUNK, BRANCH_W), F32)))
            return jnp.concatenate(outs, axis=0)

        def intra(p, v, d):
            if not latent:
                return None
            if exact:
                return intra_exact(p['scores'], v, d)
            qt, kt = p['scores']
            outs = []
            for hd in range(HEADS):
                sl = slice(hd * HEAD_DIM, (hd + 1) * HEAD_DIM)
                a = _dot_nt(qt[:, sl], kt[:, sl])
                a = jnp.where(masks[d], a, 0.0).astype(BF16)
                outs.append(_dot(a, v[:, sl]))
            return jnp.concatenate(outs, axis=-1)

        def scan_fw(p, v, branch):
            upd = {}
            for n in range(n_ch):
                rows = slice(n * CHUNK, (n + 1) * CHUNK)
                for hd in range(HEADS):
                    sl = slice(hd * HEAD_DIM, (hd + 1) * HEAD_DIM)
                    upd[n, hd] = _dot_tn(v[rows, sl], p['kd'][n][:, sl])
            st = [sfw_ref[branch * HEADS + hd] for hd in range(HEADS)]
            outs = []
            for n in range(n_ch):
                parts = []
                for hd in range(HEADS):
                    sl = slice(hd * HEAD_DIM, (hd + 1) * HEAD_DIM)
                    if latent:
                        parts.append(_dot_nt(p['qd'][n][:, sl], st[hd].astype(BF16)))
                    st[hd] = st[hd] * p['el'][n][:, sl] + upd[n, hd]
                if latent:
                    outs.append(jnp.concatenate(parts, axis=-1))
            for hd in range(HEADS):
                sfw_ref[branch * HEADS + hd] = st[hd]
            return jnp.concatenate(outs, axis=0) if latent else None

        def emit_bw(p, v, branch):
            c0 = branch * BRANCH_W
            if latent:
                for n in range(n_ch):
                    rows = slice(n * CHUNK, (n + 1) * CHUNK)
                    qdb_ref[0, rows, c0:c0 + BRANCH_W] = p['qd'][n]
                    kdb_ref[0, rows, c0:c0 + BRANCH_W] = p['kd'][n]
                    elb_ref[0, n, :, c0:c0 + BRANCH_W] = p['el'][n]
                return
            for hd in range(HEADS):
                sl = slice(hd * HEAD_DIM, (hd + 1) * HEAD_DIM)
                st = jnp.zeros((HEAD_DIM, HEAD_DIM), F32)
                for n in reversed(range(n_ch)):
                    rows = slice(n * CHUNK, (n + 1) * CHUNK)
                    st = _state_step(st, p['el'][n][:, sl], v[rows, sl], p['kd'][n][:, sl])
                sbw0_ref[0, branch * HEADS + hd] = st

        def hg_gate(raw, d):
            lbd = lb[d:d + 1]
            f = lbd + (1.0 - lbd) * _sigmoid(raw)
            return 1.0 - f, jnp.log(f)

        def total(*parts):
            return functools.reduce(lambda a, b: a + b, parts).astype(BF16)

        pq_h = proj(G_HQ) if latent else None
        pv_h = proj(G_HI)
        pf = [proj(G_HF + d) for d in range(2)]
        q_h = _silu(pq_h) if latent else None
        v_h = pv_h.astype(BF16)
        p_h0 = prep(q_h, *hg_gate(pf[0], 0), 0)
        pq_g = proj(G_GQ) if latent else None
        pk_g = proj(G_GK)
        pv_g = proj(G_GV)
        lr = _dot(h, w_ref[:, LR_COL:LR_COL + LR_PAD]).astype(BF16)
        xg = _dot(lr, wgk_ref[...]) + bgk_ref[...]
        p_h1 = prep(q_h, *hg_gate(pf[1], 1), 1)
        emit_bw(p_h1, v_h, 0)
        o0 = intra(p_h0, v_h, 0)
        q_g = pq_g * HEAD_DIM ** -0.5 if latent else None
        v_g = pv_g.astype(BF16)
        if latent:
            v_ref[0, :, 0:BRANCH_W] = v_h
            v_ref[0, :, BRANCH_W:2 * BRANCH_W] = v_g
        ls = (jnp.minimum(xg, 0.0) - jnp.log(1.0 + jnp.exp(-jnp.abs(xg)))) * (1.0 / GLA_GATE_NORM)
        p_g0 = prep(q_g, pk_g, ls[:, :BRANCH_W], 0)
        os0 = scan_fw(p_h0, v_h, 0)
        o1 = intra(p_h1, v_h, 1)
        if latent:
            oi_ref[0, :, 0:BRANCH_W] = total(o0, o1, os0)
        p_g1 = prep(q_g, pk_g, ls[:, BRANCH_W:], 1)
        emit_bw(p_g1, v_g, 1)
        o0 = intra(p_g0, v_g, 0)
        o1 = intra(p_g1, v_g, 1)
        os0 = scan_fw(p_g0, v_g, 1)
        if latent:
            oi_ref[0, :, BRANCH_W:2 * BRANCH_W] = total(o0, o1, os0)

        widest = jnp.zeros((1, BRANCH_W), F32)
        for piece in span:
            widest = jnp.maximum(widest, piece)
        lanes = widest[:, 0:HEAD_DIM]
        for hd in range(1, HEADS):
            lanes = jnp.maximum(lanes, widest[:, hd * HEAD_DIM:(hd + 1) * HEAD_DIM])
        rng_ref[0, 0] = lanes

    j = pl.program_id(1)

    @pl.when(j == 0)
    def _():
        body(ctx_ref, False)

    @pl.when(j > 0)
    def _():
        body(x_ref, True)


def _in_projection(x, ctx, mod3, norm_pre, w_in_r, hg_lb, wgk, bgk, cast_weights, *, tm, exact):
    b, seq, d = x.shape
    ctx_len = ctx.shape[1]
    assert ctx_len == tm and tm % CHUNK == 0
    n_lat = seq // tm
    n_ch = tm // CHUNK
    nw = w_in_r.shape[1]
    n_cast = len(cast_weights)
    cast_2d = [cw.reshape(cw.shape[-2:]) for cw in cast_weights]
    kern = functools.partial(_inproj_kernel, d_model=d, tm=tm, exact=exact, n_cast=n_cast)

    def cast_spec(cw):
        tiles = cw.shape[0] // CAST_ROW_TILE
        n_slabs = max(n for n in range(1, b * n_lat + 1) if tiles % n == 0)
        return pl.BlockSpec(
            (cw.shape[0] // n_slabs, cw.shape[1]),
            lambda i, j: (jnp.minimum(i * n_lat + jnp.maximum(j - 1, 0), n_slabs - 1), 0))

    def latent_rows(w):
        return pl.BlockSpec((1, tm, w), lambda i, j: (i, jnp.maximum(j - 1, 0), 0))

    n_state = 2 * HEADS
    return pl.pallas_call(
        kern,
        grid=(b, n_lat + 1),
        in_specs=[
            latent_rows(d),
            pl.BlockSpec((1, tm, d), lambda i, j: (i, 0, 0)),
            pl.BlockSpec((1, 1, N_MOD * d), lambda i, j: (jnp.where(j == 0, 0, i + 1), 0, 0)),
            _const_spec((1, d)),
            _const_spec((d, nw)),
            _const_spec(hg_lb.shape),
            _const_spec(wgk.shape),
            _const_spec(bgk.shape),
        ] + [cast_spec(cw) for cw in cast_2d],
        out_specs=[
            latent_rows(DIR_W),
            latent_rows(DIR_W),
            latent_rows(DIR_W),
            pl.BlockSpec((1, n_ch, 1, DIR_W), lambda i, j: (i, jnp.maximum(j - 1, 0), 0, 0)),
            latent_rows(DIR_W),
            pl.BlockSpec((1, 1, 1, HEAD_DIM), lambda i, j: (i, j, 0, 0)),
            pl.BlockSpec((1, n_state, HEAD_DIM, HEAD_DIM), lambda i, j: (i, 0, 0, 0)),
        ] + [cast_spec(cw) for cw in cast_2d],
        out_shape=[
            jax.ShapeDtypeStruct((b, seq, DIR_W), BF16),
            jax.ShapeDtypeStruct((b, seq, DIR_W), BF16),
            jax.ShapeDtypeStruct((b, seq, DIR_W), BF16),
            jax.ShapeDtypeStruct((b, seq // CHUNK, 1, DIR_W), F32),
            jax.ShapeDtypeStruct((b, seq, DIR_W), BF16),
            jax.ShapeDtypeStruct((b, n_lat + 1, 1, HEAD_DIM), F32),
            jax.ShapeDtypeStruct((b, n_state, HEAD_DIM, HEAD_DIM), F32),
        ] + [jax.ShapeDtypeStruct(cw.shape, BF16) for cw in cast_2d],
        scratch_shapes=([pltpu.VMEM((n_state, HEAD_DIM, HEAD_DIM), F32)]
                        + ([pltpu.VMEM((tm, BRANCH_W), F32)] * 3 if exact else [])),
        compiler_params=pltpu.CompilerParams(
            dimension_semantics=("arbitrary", "arbitrary"),
            vmem_limit_bytes=VMEM_LIMIT_BYTES),
        name="in_projection_exact" if exact else "in_projection",
    )(x, ctx, mod3, norm_pre, w_in_r, hg_lb, wgk, bgk, *cast_2d)


def _tail_kernel(oi_ref, qdb_ref, kdb_ref, v_ref, elb_ref, sbw0_ref, x_ref, mod_ref,
                 hgn_ref, gln_ref, pre1_ref, post1_ref, pre2_ref, post2_ref,
                 wgh_ref, wgg_ref, wgm_ref, wbh_ref, wbg_ref, wout_ref, wg_ref, wu_ref, wd_ref,
                 out_ref, sbw_ref, *, d_model, n_streams):
    mod = mod_ref[0]
    m = [mod[:, i * d_model:(i + 1) * d_model] for i in range(N_MOD)]
    rows_per_stream = x_ref.shape[1] // n_streams
    chunks_per_stream = rows_per_stream // CHUNK
    gates = (wgh_ref, wgg_ref)
    norms = (hgn_ref, gln_ref)
    branch_w = (wbh_ref, wbg_ref)
    streams = [dict(first=i * rows_per_stream, rows=slice(i * rows_per_stream, (i + 1) * rows_per_stream))
               for i in range(n_streams)]

    @pl.when(pl.program_id(1) == 0)
    def _():
        sbw_ref[...] = sbw0_ref[0]

    def scan_bw(t):
        parts = [None] * chunks_per_stream
        for n in reversed(range(chunks_per_stream)):
            rows = slice(t['first'] + n * CHUNK, t['first'] + (n + 1) * CHUNK)
            e_last = elb_ref[0, t['first'] // CHUNK + n]
            heads = []
            for u in range(2 * HEADS):
                sl = slice(u * HEAD_DIM, (u + 1) * HEAD_DIM)
                st = sbw_ref[u]
                heads.append(_dot_nt(qdb_ref[0, rows, sl], st.astype(BF16)))
                sbw_ref[u] = _state_step(st, e_last[:, sl], v_ref[0, rows, sl], kdb_ref[0, rows, sl])
            parts[n] = jnp.concatenate(heads, axis=-1)
        t['o_bw'] = jnp.concatenate(parts, axis=0)

    def mixer_input(t):
        t['x'] = x_ref[0, t['rows'], :]
        t['h1'] = (_rms(t['x']) * pre1_ref[...] * (1.0 + m[1]) + m[0]).astype(BF16)

    def gate_proj(t):
        t['out_gate'] = [_silu(_dot(t['h1'], gates[br][...])) for br in range(2)]
        t['merge_gate'] = _sigmoid(_dot(t['h1'], wgm_ref[...]))

    def branch_proj(t):
        o = oi_ref[0, t['rows'], :].astype(F32) + t['o_bw']
        t['ys'] = []
        for br in range(2):
            parts = []
            for hd in range(HEADS):
                c0 = br * BRANCH_W + hd * HEAD_DIM
                parts.append(_rms(o[:, c0:c0 + HEAD_DIM]) * norms[br][...])
            og = (jnp.concatenate(parts, axis=-1) * t['out_gate'][br]).astype(BF16)
            t['ys'].append(_dot(og, branch_w[br][...]))

    def out_proj(t):
        gm = t['merge_gate']
        merged = gm[:, :d_model] * t['ys'][0] + gm[:, d_model:] * t['ys'][1]
        t['y'] = _dot(merged.astype(BF16), wout_ref[...])

    def ffn_input(t):
        t['z1'] = t['x'] + _rms(t['y']) * post1_ref[...] * m[2]
        t['h2'] = (_rms(t['z1']) * pre2_ref[...] * (1.0 + m[4]) + m[3]).astype(BF16)

    def ffn_up(t):
        g = _dot(t['h2'], wg_ref[...])
        u = _dot(t['h2'], wu_ref[...])
        t['a'] = (_silu(g) * u).astype(BF16)

    def ffn_down(t):
        t['y2'] = _dot(t['a'], wd_ref[...])

    def residual_out(t):
        out_ref[0, t['rows'], :] = t['z1'] + _rms(t['y2']) * post2_ref[...] * m[5]

    for t in reversed(streams):
        scan_bw(t)
    for stage in (mixer_input, gate_proj, branch_proj, out_proj, ffn_input, ffn_up, ffn_down, residual_out):
        for t in streams:
            stage(t)


def _tail(oi, qdb, kdb, v, elb, sbw0, x, mod3, *consts, tm):
    b, seq, d = x.shape
    n_tiles = seq // tm
    kern = functools.partial(_tail_kernel, d_model=d, n_streams=TAIL_STREAMS)

    def rows(w):
        return pl.BlockSpec((1, tm, w), lambda i, j: (i, n_tiles - 1 - j, 0))

    return pl.pallas_call(
        kern,
        grid=(b, n_tiles),
        in_specs=[
            rows(DIR_W), rows(DIR_W), rows(DIR_W), rows(DIR_W),
            pl.BlockSpec((1, tm // CHUNK, 1, DIR_W), lambda i, j: (i, n_tiles - 1 - j, 0, 0)),
            pl.BlockSpec((1,) + sbw0.shape[1:], lambda i, j: (i, 0, 0, 0)),
            rows(d),
            pl.BlockSpec((1, 1, N_MOD * d), lambda i, j: (i + 1, 0, 0)),
        ] + [_const_spec(c.shape) for c in consts],
        out_specs=rows(d),
        out_shape=jax.ShapeDtypeStruct((b, seq, d), F32),
        scratch_shapes=[pltpu.VMEM(sbw0.shape[1:], F32)],
        compiler_params=pltpu.CompilerParams(
            dimension_semantics=("arbitrary", "arbitrary"),
            vmem_limit_bytes=VMEM_LIMIT_BYTES),
        name="mixer_out_ffn",
    )(oi, qdb, kdb, v, elb, sbw0, x, mod3, *consts)


def kernel(x, c, ctx, c_ctx, w_mod, b_mod, norm_pre1, norm_post1, norm_pre2, norm_post2, w_in, hg_lb,
           hg_onorm, gla_w_gk, gla_b_gk, gla_onorm, w_br_hg, w_br_gla, w_out, w_ff_gate, w_ff_up, w_ff_down):
    b, seq, d = x.shape
    ctx_len = ctx.shape[1]
    assert w_mod.shape[0] == 1 and d == DIR_W
    assert seq % CHUNK == 0 and ctx_len % CHUNK == 0
    tm = ctx_len

    c_all = jnp.zeros((MOD_ROWS, d), F32).at[0].set(c_ctx).at[1:b + 1].set(c)
    mod3 = _modulation(c_all, w_mod.reshape(d, N_MOD * d), b_mod).reshape(MOD_ROWS, 1, N_MOD * d)

    def bf(a):
        return a.reshape(a.shape[1:]).astype(BF16)

    w = bf(w_in)
    lr_w = 2 * GLA_RANK
    wgk = jnp.zeros((LR_PAD, 2 * BRANCH_W), F32)
    wgk = wgk.at[:GLA_RANK, :BRANCH_W].set(gla_w_gk[0, 0]).at[GLA_RANK:lr_w, BRANCH_W:].set(gla_w_gk[0, 1])
    bgk = gla_b_gk.reshape(1, 2 * BRANCH_W)

    tail_weights = (w_br_hg, w_br_gla, w_out, w_ff_gate, w_ff_up, w_ff_down)

    def in_projection(exact):
        return _in_projection(x, ctx, mod3, norm_pre1, w, hg_lb, wgk.astype(BF16), bgk,
                              () if exact else tail_weights, tm=tm, exact=exact)

    *fast, excursion, sbw0, wbh, wbg, wo, wfg, wfu, wfd = in_projection(False)
    tail_bf16 = (wbh, wbg, wo, wfg, wfu, wfd)

    def rest(qdb, kdb, v, elb, oi, sbw0):
        return _tail(oi, qdb, kdb, v, elb, sbw0, x, mod3,
                     hg_onorm, gla_onorm, norm_pre1, norm_post1, norm_pre2, norm_post2,
                     w[:, G_HGATE * BRANCH_W:(G_HGATE + 1) * BRANCH_W],
                     w[:, G_GGATE * BRANCH_W:(G_GGATE + 1) * BRANCH_W],
                     w[:, LR_COL + lr_w:],
                     *tail_bf16, tm=TAIL_ROWS)

    def redo_exact():
        qdb, kdb, v, elb, oi, _, sbw0_exact = in_projection(True)[:7]
        return rest(qdb, kdb, v, elb, oi, sbw0_exact)

    return lax.cond(jnp.max(excursion) > EXCURSION_LIMIT_LOG2, redo_exact, lambda: rest(*fast, sbw0))
```

```python
import functools

import jax
import jax.numpy as jnp
from jax import lax
from jax.experimental import pallas as pl
from jax.experimental.pallas import tpu as pltpu

F32 = jnp.float32
BF16 = jnp.bfloat16

EPS = 1e-6
LOG2E = 1.4426950408889634
CHUNK = 64
N_MOD = 6
HEADS = 4
HEAD_DIM = 128
BRANCH_W = HEADS * HEAD_DIM
DIR_W = 2 * BRANCH_W
GLA_RANK = 16
GLA_GATE_NORM = 16.0
LR_PAD = 128
G_HQ, G_HI, G_HF, G_HB, G_HGATE, G_GQ, G_GK, G_GV, G_GGATE = range(9)
LR_COL = 9 * BRANCH_W
MOD_ROWS = 16
VMEM_LIMIT_BYTES = 56 * 1024 * 1024
TAIL_ROWS = 512
TAIL_STREAMS = 2
IN_STREAMS = 2
CAST_ROW_TILE = 16
EXCURSION_LIMIT_LOG2 = 96.0


def _dot(a, b):
    return jnp.dot(a, b, preferred_element_type=F32)


def _dot_nt(a, b):
    return lax.dot_general(a, b, (((1,), (1,)), ((), ())), preferred_element_type=F32)


def _dot_tn(a, b):
    return lax.dot_general(a, b, (((0,), (0,)), ((), ())), preferred_element_type=F32)


def _sigmoid(x):
    return 1.0 / (1.0 + jnp.exp(-x))


def _silu(x):
    return x * _sigmoid(x)


def _rms(x):
    return x * lax.rsqrt(jnp.mean(x * x, axis=-1, keepdims=True) + EPS)


def _const_spec(shape):
    nd = len(shape)
    return pl.BlockSpec(shape, lambda *_: (0,) * nd, pipeline_mode=pl.Buffered(1))


def _mod_kernel(c_ref, w_ref, b_ref, o_ref):
    a = _silu(c_ref[...]).astype(BF16)
    o_ref[...] = _dot(a, w_ref[...].astype(BF16)) + b_ref[...]


def _modulation(c_all, w_mod, b_mod):
    d = c_all.shape[1]
    n = w_mod.shape[1]
    blk = 1024
    return pl.pallas_call(
        _mod_kernel,
        grid=(n // blk,),
        in_specs=[pl.BlockSpec((MOD_ROWS, d), lambda j: (0, 0)),
                  pl.BlockSpec((d, blk), lambda j: (0, j)),
                  pl.BlockSpec((1, blk), lambda j: (0, j))],
        out_specs=pl.BlockSpec((MOD_ROWS, blk), lambda j: (0, j)),
        out_shape=jax.ShapeDtypeStruct((MOD_ROWS, n), F32),
        name="modulation",
    )(c_all, w_mod, b_mod)


def _state_step(st, e_last, v, kd):
    return st * e_last + _dot_tn(v, kd)


def _inproj_kernel(x_ref, ctx_ref, mod_ref, pre_ref, w_ref, lb_ref, wgk_ref, bgk_ref, *rest,
                   d_model, rows_per_stream, n_streams, exact, n_cast):
    cast_in, rest = rest[:n_cast], rest[n_cast:]
    qdb_ref, kdb_ref, v_ref, elb_ref, oi_ref, rng_ref, sbw0_ref = rest[:7]
    cast_out, rest = rest[7:7 + n_cast], rest[7 + n_cast:]
    sfw_ref, scratch = rest[0], rest[1:]
    rs = rows_per_stream
    n_ch = rs // CHUNK
    mids = (CHUNK // 2, CHUNK // 2 - 1)
    lasts = (CHUNK - 1, 0)

    def body(z_ref, latent):
        if latent:
            for wi_ref, wo_ref in zip(cast_in, cast_out):
                wo_ref[...] = wi_ref[...].astype(BF16)
        else:
            sfw_ref[...] = jnp.zeros_like(sfw_ref)
        mod = mod_ref[0]
        shift, scale = mod[:, :d_model], mod[:, d_model:2 * d_model]

        lbr = lb_ref[...]
        e = jnp.exp(lbr - jnp.max(lbr, axis=0, keepdims=True))
        lb = e[0] / jnp.sum(e, axis=0)

        row = lax.broadcasted_iota(jnp.int32, (rs, rs), 0)
        col = lax.broadcasted_iota(jnp.int32, (rs, rs), 1)
        shift_bits = CHUNK.bit_length() - 1
        same = jnp.right_shift(row, shift_bits) == jnp.right_shift(col, shift_bits)
        masks = (same & (col <= row), same & (col >= row))
        tris = tuple(jnp.where(m, 1.0, 0.0).astype(BF16) for m in masks)

        def proj(t, g):
            return _dot(t['h'], w_ref[:, g * BRANCH_W:(g + 1) * BRANCH_W])

        span = []

        def prep(q, k, lf, d):
            lf2 = lf * LOG2E
            hi = lf2.astype(BF16)
            md = (lf2 - hi.astype(F32)).astype(BF16)
            c = _dot(tris[d], hi) + _dot(tris[d], md)
            qts, kts, qds, kds, els = [], [], [], [], []
            for n in range(n_ch):
                rows = slice(n * CHUNK, (n + 1) * CHUNK)
                cj = c[rows]
                cl = cj[lasts[d]:lasts[d] + 1]
                els.append(jnp.exp2(cl))
                if exact or not latent:
                    kds.append((k[rows] * jnp.exp2(cl - cj)).astype(BF16))
                    if latent:
                        qds.append((q[rows] * jnp.exp2(cj)).astype(BF16))
                    continue
                r = cj[mids[d]:mids[d] + 1]
                dq = cj - r
                span.append(jnp.maximum(jnp.abs(dq[0:1]), jnp.abs(dq[CHUNK - 1:CHUNK])))
                qt = q[rows] * jnp.exp2(dq)
                kt = k[rows] * jnp.exp2(-dq)
                qds.append((qt * jnp.exp2(r)).astype(BF16))
                kds.append((kt * jnp.exp2(cl - r)).astype(BF16))
                qts.append(qt.astype(BF16))
                kts.append(kt.astype(BF16))
            scores = None
            if latent:
                scores = (q, k, c) if exact else (jnp.concatenate(qts, axis=0), jnp.concatenate(kts, axis=0))
            return dict(qd=qds, kd=kds, el=els, scores=scores)

        def intra_exact(qkc, v, d):
            q, k, c = qkc
            k_s, v_s, c_s = scratch
            k_s[...] = k
            v_s[...] = v.astype(F32)
            c_s[...] = c
            rowid = lax.broadcasted_iota(jnp.int32, (CHUNK, 1), 0)
            outs = []
            for n in range(n_ch):
                base = n * CHUNK
                qj = q[base:base + CHUNK]
                cj = c[base:base + CHUNK]

                def step(s, acc):
                    cs = c_s[pl.ds(base + s, 1), :]
                    ks = k_s[pl.ds(base + s, 1), :]
                    vs = v_s[pl.ds(base + s, 1), :]
                    at_or_after = (rowid >= s) if d == 0 else (rowid <= s)
                    p = jnp.where(at_or_after, qj * jnp.exp2(jnp.minimum(cj - cs, 0.0)) * ks, 0.0)
                    parts = []
                    for hd in range(HEADS):
                        sl = slice(hd * HEAD_DIM, (hd + 1) * HEAD_DIM)
                        parts.append(jnp.sum(p[:, sl], axis=-1, keepdims=True) * vs[:, sl])
                    return acc + jnp.concatenate(parts, axis=-1)

                outs.append(lax.fori_loop(0, CHUNK, step, jnp.zeros((CHUNK, BRANCH_W), F32)))
            return jnp.concatenate(outs, axis=0)

        def intra(p, v, d):
            if not latent:
                return None
            if exact:
                return intra_exact(p['scores'], v, d)
            qt, kt = p['scores']
            outs = []
            for hd in range(HEADS):
                sl = slice(hd * HEAD_DIM, (hd + 1) * HEAD_DIM)
                a = _dot_nt(qt[:, sl], kt[:, sl])
                a = jnp.where(masks[d], a, 0.0).astype(BF16)
                outs.append(_dot(a, v[:, sl]))
            return jnp.concatenate(outs, axis=-1)

        def scan_fw(p, v, branch):
            upd = {}
            for n in range(n_ch):
                rows = slice(n * CHUNK, (n + 1) * CHUNK)
                for hd in range(HEADS):
                    sl = slice(hd * HEAD_DIM, (hd + 1) * HEAD_DIM)
                    upd[n, hd] = _dot_tn(v[rows, sl], p['kd'][n][:, sl])
            st = [sfw_ref[branch * HEADS + hd] for hd in range(HEADS)]
            outs = []
            for n in range(n_ch):
                parts = []
                for hd in range(HEADS):
                    sl = slice(hd * HEAD_DIM, (hd + 1) * HEAD_DIM)
                    if latent:
                        parts.append(_dot_nt(p['qd'][n][:, sl], st[hd].astype(BF16)))
                    st[hd] = st[hd] * p['el'][n][:, sl] + upd[n, hd]
                if latent:
                    outs.append(jnp.concatenate(parts, axis=-1))
            for hd in range(HEADS):
                sfw_ref[branch * HEADS + hd] = st[hd]
            return jnp.concatenate(outs, axis=0) if latent else None

        def emit_bw(t, p, v, branch):
            c0 = branch * BRANCH_W
            if latent:
                for n in range(n_ch):
                    rows = slice(t['first'] + n * CHUNK, t['first'] + (n + 1) * CHUNK)
                    qdb_ref[0, rows, c0:c0 + BRANCH_W] = p['qd'][n]
                    kdb_ref[0, rows, c0:c0 + BRANCH_W] = p['kd'][n]
                    elb_ref[0, t['first'] // CHUNK + n, :, c0:c0 + BRANCH_W] = p['el'][n]
                return
            for hd in range(HEADS):
                sl = slice(hd * HEAD_DIM, (hd + 1) * HEAD_DIM)
                st = jnp.zeros((HEAD_DIM, HEAD_DIM), F32)
                for n in reversed(range(n_ch)):
                    rows = slice(n * CHUNK, (n + 1) * CHUNK)
                    st = _state_step(st, p['el'][n][:, sl], v[rows, sl], p['kd'][n][:, sl])
                sbw0_ref[0, branch * HEADS + hd] = st

        def hg_gate(raw, d):
            lbd = lb[d:d + 1]
            f = lbd + (1.0 - lbd) * _sigmoid(raw)
            return 1.0 - f, jnp.log(f)

        def total(*parts):
            return functools.reduce(lambda a, b: a + b, parts).astype(BF16)

        def s_input(t):
            t['h'] = (_rms(z_ref[0, t['rows'], :]) * pre_ref[...] * (1.0 + scale) + shift).astype(BF16)

        def s_hg_proj(t):
            pq_h = proj(t, G_HQ) if latent else None
            pv_h = proj(t, G_HI)
            t['pf'] = [proj(t, G_HF + d) for d in range(2)]
            t['q_h'] = _silu(pq_h) if latent else None
            t['v_h'] = pv_h.astype(BF16)

        def s_hg_fw(t):
            t['p_h0'] = prep(t['q_h'], *hg_gate(t['pf'][0], 0), 0)

        def s_gla_proj(t):
            t['pq_g'] = proj(t, G_GQ) if latent else None
            t['pk_g'] = proj(t, G_GK)
            t['pv_g'] = proj(t, G_GV)
            lr = _dot(t['h'], w_ref[:, LR_COL:LR_COL + LR_PAD]).astype(BF16)
            t['xg'] = _dot(lr, wgk_ref[...]) + bgk_ref[...]

        def s_hg_bw(t):
            t['p_h1'] = prep(t['q_h'], *hg_gate(t['pf'][1], 1), 1)
            emit_bw(t, t['p_h1'], t['v_h'], 0)

        def s_hg_intra_fw(t):
            t['o0'] = intra(t['p_h0'], t['v_h'], 0)

        def s_gla_fw(t):
            t['q_g'] = t['pq_g'] * HEAD_DIM ** -0.5 if latent else None
            t['v_g'] = t['pv_g'].astype(BF16)
            if latent:
                v_ref[0, t['rows'], 0:BRANCH_W] = t['v_h']
                v_ref[0, t['rows'], BRANCH_W:2 * BRANCH_W] = t['v_g']
            xg = t['xg']
            t['ls'] = (jnp.minimum(xg, 0.0) - jnp.log(1.0 + jnp.exp(-jnp.abs(xg)))) * (1.0 / GLA_GATE_NORM)
            t['p_g0'] = prep(t['q_g'], t['pk_g'], t['ls'][:, :BRANCH_W], 0)

        def s_hg_finish(t):
            os0 = scan_fw(t['p_h0'], t['v_h'], 0)
            o1 = intra(t['p_h1'], t['v_h'], 1)
            if latent:
                oi_ref[0, t['rows'], 0:BRANCH_W] = total(t['o0'], o1, os0)

        def s_gla_bw(t):
            t['p_g1'] = prep(t['q_g'], t['pk_g'], t['ls'][:, BRANCH_W:], 1)
            emit_bw(t, t['p_g1'], t['v_g'], 1)

        def s_gla_finish(t):
            o0 = intra(t['p_g0'], t['v_g'], 0)
            o1 = intra(t['p_g1'], t['v_g'], 1)
            os0 = scan_fw(t['p_g0'], t['v_g'], 1)
            if latent:
                oi_ref[0, t['rows'], BRANCH_W:2 * BRANCH_W] = total(o0, o1, os0)

        n_here = n_streams if latent else 1
        streams = [dict(first=i * rs, rows=slice(i * rs, (i + 1) * rs)) for i in range(n_here)]
        for stage in (s_input, s_hg_proj, s_hg_fw, s_gla_proj, s_hg_bw, s_hg_intra_fw, s_gla_fw, s_hg_finish,
                      s_gla_bw, s_gla_finish):
            for t in streams:
                stage(t)

        widest = jnp.zeros((1, BRANCH_W), F32)
        for piece in span:
            widest = jnp.maximum(widest, piece)
        lanes = widest[:, 0:HEAD_DIM]
        for hd in range(1, HEADS):
            lanes = jnp.maximum(lanes, widest[:, hd * HEAD_DIM:(hd + 1) * HEAD_DIM])
        rng_ref[0, 0] = lanes

    j = pl.program_id(1)

    @pl.when(j == 0)
    def _():
        body(ctx_ref, False)

    @pl.when(j > 0)
    def _():
        body(x_ref, True)


def _in_projection(x, ctx, mod3, norm_pre, w_in_r, hg_lb, wgk, bgk, cast_weights, *, n_streams, exact):
    b, seq, d = x.shape
    rs = ctx.shape[1]
    tm = n_streams * rs
    assert rs % CHUNK == 0 and seq % tm == 0
    n_lat = seq // tm
    nw = w_in_r.shape[1]
    n_cast = len(cast_weights)
    cast_2d = [cw.reshape(cw.shape[-2:]) for cw in cast_weights]
    kern = functools.partial(_inproj_kernel, d_model=d, rows_per_stream=rs, n_streams=n_streams,
                             exact=exact, n_cast=n_cast)

    def cast_spec(cw):
        tiles = cw.shape[0] // CAST_ROW_TILE
        n_slabs = max(n for n in range(1, b * n_lat + 1) if tiles % n == 0)
        return pl.BlockSpec(
            (cw.shape[0] // n_slabs, cw.shape[1]),
            lambda i, j: (jnp.minimum(i * n_lat + jnp.maximum(j - 1, 0), n_slabs - 1), 0))

    def latent_rows(w):
        return pl.BlockSpec((1, tm, w), lambda i, j: (i, jnp.maximum(j - 1, 0), 0))

    n_state = 2 * HEADS
    return pl.pallas_call(
        kern,
        grid=(b, n_lat + 1),
        in_specs=[
            latent_rows(d),
            pl.BlockSpec((1, rs, d), lambda i, j: (i, 0, 0)),
            pl.BlockSpec((1, 1, N_MOD * d), lambda i, j: (jnp.where(j == 0, 0, i + 1), 0, 0)),
            _const_spec((1, d)),
            _const_spec((d, nw)),
            _const_spec(hg_lb.shape),
            _const_spec(wgk.shape),
            _const_spec(bgk.shape),
        ] + [cast_spec(cw) for cw in cast_2d],
        out_specs=[
            latent_rows(DIR_W),
            latent_rows(DIR_W),
            latent_rows(DIR_W),
            pl.BlockSpec((1, tm // CHUNK, 1, DIR_W), lambda i, j: (i, jnp.maximum(j - 1, 0), 0, 0)),
            latent_rows(DIR_W),
            pl.BlockSpec((1, 1, 1, HEAD_DIM), lambda i, j: (i, j, 0, 0)),
            pl.BlockSpec((1, n_state, HEAD_DIM, HEAD_DIM), lambda i, j: (i, 0, 0, 0)),
        ] + [cast_spec(cw) for cw in cast_2d],
        out_shape=[
            jax.ShapeDtypeStruct((b, seq, DIR_W), BF16),
            jax.ShapeDtypeStruct((b, seq, DIR_W), BF16),
            jax.ShapeDtypeStruct((b, seq, DIR_W), BF16),
            jax.ShapeDtypeStruct((b, seq // CHUNK, 1, DIR_W), F32),
            jax.ShapeDtypeStruct((b, seq, DIR_W), BF16),
            jax.ShapeDtypeStruct((b, n_lat + 1, 1, HEAD_DIM), F32),
            jax.ShapeDtypeStruct((b, n_state, HEAD_DIM, HEAD_DIM), F32),
        ] + [jax.ShapeDtypeStruct(cw.shape, BF16) for cw in cast_2d],
        scratch_shapes=([pltpu.VMEM((n_state, HEAD_DIM, HEAD_DIM), F32)]
                        + ([pltpu.VMEM((rs, BRANCH_W), F32)] * 3 if exact else [])),
        compiler_params=pltpu.CompilerParams(
            dimension_semantics=("arbitrary", "arbitrary"),
            vmem_limit_bytes=VMEM_LIMIT_BYTES),
        name="in_projection_exact" if exact else "in_projection",
    )(x, ctx, mod3, norm_pre, w_in_r, hg_lb, wgk, bgk, *cast_2d)


def _tail_kernel(oi_ref, qdb_ref, kdb_ref, v_ref, elb_ref, sbw0_ref, x_ref, mod_ref,
                 hgn_ref, gln_ref, pre1_ref, post1_ref, pre2_ref, post2_ref,
                 wgh_ref, wgg_ref, wgm_ref, wbh_ref, wbg_ref, wout_ref, wg_ref, wu_ref, wd_ref,
                 out_ref, sbw_ref, *, d_model, n_streams):
    mod = mod_ref[0]
    m = [mod[:, i * d_model:(i + 1) * d_model] for i in range(N_MOD)]
    rows_per_stream = x_ref.shape[1] // n_streams
    chunks_per_stream = rows_per_stream // CHUNK
    gates = (wgh_ref, wgg_ref)
    norms = (hgn_ref, gln_ref)
    branch_w = (wbh_ref, wbg_ref)
    streams = [dict(first=i * rows_per_stream, rows=slice(i * rows_per_stream, (i + 1) * rows_per_stream))
               for i in range(n_streams)]

    @pl.when(pl.program_id(1) == 0)
    def _():
        sbw_ref[...] = sbw0_ref[0]

    def scan_bw(t):
        parts = [None] * chunks_per_stream
        for n in reversed(range(chunks_per_stream)):
            rows = slice(t['first'] + n * CHUNK, t['first'] + (n + 1) * CHUNK)
            e_last = elb_ref[0, t['first'] // CHUNK + n]
            heads = []
            for u in range(2 * HEADS):
                sl = slice(u * HEAD_DIM, (u + 1) * HEAD_DIM)
                st = sbw_ref[u]
                heads.append(_dot_nt(qdb_ref[0, rows, sl], st.astype(BF16)))
                sbw_ref[u] = _state_step(st, e_last[:, sl], v_ref[0, rows, sl], kdb_ref[0, rows, sl])
            parts[n] = jnp.concatenate(heads, axis=-1)
        t['o_bw'] = jnp.concatenate(parts, axis=0)

    def mixer_input(t):
        t['x'] = x_ref[0, t['rows'], :]
        t['h1'] = (_rms(t['x']) * pre1_ref[...] * (1.0 + m[1]) + m[0]).astype(BF16)

    def gate_proj(t):
        t['out_gate'] = [_silu(_dot(t['h1'], gates[br][...])) for br in range(2)]
        t['merge_gate'] = _sigmoid(_dot(t['h1'], wgm_ref[...]))

    def branch_proj(t):
        o = oi_ref[0, t['rows'], :].astype(F32) + t['o_bw']
        t['ys'] = []
        for br in range(2):
            parts = []
            for hd in range(HEADS):
                c0 = br * BRANCH_W + hd * HEAD_DIM
                parts.append(_rms(o[:, c0:c0 + HEAD_DIM]) * norms[br][...])
            og = (jnp.concatenate(parts, axis=-1) * t['out_gate'][br]).astype(BF16)
            t['ys'].append(_dot(og, branch_w[br][...]))

    def out_proj(t):
        gm = t['merge_gate']
        merged = gm[:, :d_model] * t['ys'][0] + gm[:, d_model:] * t['ys'][1]
        t['y'] = _dot(merged.astype(BF16), wout_ref[...])

    def ffn_input(t):
        t['z1'] = t['x'] + _rms(t['y']) * post1_ref[...] * m[2]
        t['h2'] = (_rms(t['z1']) * pre2_ref[...] * (1.0 + m[4]) + m[3]).astype(BF16)

    def ffn_up(t):
        g = _dot(t['h2'], wg_ref[...])
        u = _dot(t['h2'], wu_ref[...])
        t['a'] = (_silu(g) * u).astype(BF16)

    def ffn_down(t):
        t['y2'] = _dot(t['a'], wd_ref[...])

    def residual_out(t):
        out_ref[0, t['rows'], :] = t['z1'] + _rms(t['y2']) * post2_ref[...] * m[5]

    for t in reversed(streams):
        scan_bw(t)
    for stage in (mixer_input, gate_proj, branch_proj, out_proj, ffn_input, ffn_up, ffn_down, residual_out):
        for t in streams:
            stage(t)


def _tail(oi, qdb, kdb, v, elb, sbw0, x, mod3, *consts, tm):
    b, seq, d = x.shape
    n_tiles = seq // tm
    kern = functools.partial(_tail_kernel, d_model=d, n_streams=TAIL_STREAMS)

    def rows(w):
        return pl.BlockSpec((1, tm, w), lambda i, j: (i, n_tiles - 1 - j, 0))

    return pl.pallas_call(
        kern,
        grid=(b, n_tiles),
        in_specs=[
            rows(DIR_W), rows(DIR_W), rows(DIR_W), rows(DIR_W),
            pl.BlockSpec((1, tm // CHUNK, 1, DIR_W), lambda i, j: (i, n_tiles - 1 - j, 0, 0)),
            pl.BlockSpec((1,) + sbw0.shape[1:], lambda i, j: (i, 0, 0, 0)),
            rows(d),
            pl.BlockSpec((1, 1, N_MOD * d), lambda i, j: (i + 1, 0, 0)),
        ] + [_const_spec(c.shape) for c in consts],
        out_specs=rows(d),
        out_shape=jax.ShapeDtypeStruct((b, seq, d), F32),
        scratch_shapes=[pltpu.VMEM(sbw0.shape[1:], F32)],
        compiler_params=pltpu.CompilerParams(
            dimension_semantics=("arbitrary", "arbitrary"),
            vmem_limit_bytes=VMEM_LIMIT_BYTES),
        name="mixer_out_ffn",
    )(oi, qdb, kdb, v, elb, sbw0, x, mod3, *consts)


def kernel(x, c, ctx, c_ctx, w_mod, b_mod, norm_pre1, norm_post1, norm_pre2, norm_post2, w_in, hg_lb,
           hg_onorm, gla_w_gk, gla_b_gk, gla_onorm, w_br_hg, w_br_gla, w_out, w_ff_gate, w_ff_up, w_ff_down):
    b, seq, d = x.shape
    ctx_len = ctx.shape[1]
    assert w_mod.shape[0] == 1 and d == DIR_W
    assert seq % CHUNK == 0 and ctx_len % CHUNK == 0

    c_all = jnp.zeros((MOD_ROWS, d), F32).at[0].set(c_ctx).at[1:b + 1].set(c)
    mod3 = _modulation(c_all, w_mod.reshape(d, N_MOD * d), b_mod).reshape(MOD_ROWS, 1, N_MOD * d)

    def bf(a):
        return a.reshape(a.shape[1:]).astype(BF16)

    w = bf(w_in)
    lr_w = 2 * GLA_RANK
    wgk = jnp.zeros((LR_PAD, 2 * BRANCH_W), F32)
    wgk = wgk.at[:GLA_RANK, :BRANCH_W].set(gla_w_gk[0, 0]).at[GLA_RANK:lr_w, BRANCH_W:].set(gla_w_gk[0, 1])
    bgk = gla_b_gk.reshape(1, 2 * BRANCH_W)

    tail_weights = (w_br_hg, w_br_gla, w_out, w_ff_gate, w_ff_up, w_ff_down)

    def in_projection(exact):
        return _in_projection(x, ctx, mod3, norm_pre1, w, hg_lb, wgk.astype(BF16), bgk,
                              () if exact else tail_weights, n_streams=IN_STREAMS, exact=exact)

    *fast, excursion, sbw0, wbh, wbg, wo, wfg, wfu, wfd = in_projection(False)
    tail_bf16 = (wbh, wbg, wo, wfg, wfu, wfd)

    def rest(qdb, kdb, v, elb, oi, sbw0):
        return _tail(oi, qdb, kdb, v, elb, sbw0, x, mod3,
                     hg_onorm, gla_onorm, norm_pre1, norm_post1, norm_pre2, norm_post2,
                     w[:, G_HGATE * BRANCH_W:(G_HGATE + 1) * BRANCH_W],
                     w[:, G_GGATE * BRANCH_W:(G_GGATE + 1) * BRANCH_W],
                     w[:, LR_COL + lr_w:],
                     *tail_bf16, tm=TAIL_ROWS)

    def redo_exact():
        qdb, kdb, v, elb, oi, _, sbw0_exact = in_projection(True)[:7]
        return rest(qdb, kdb, v, elb, oi, sbw0_exact)

    return lax.cond(jnp.max(excursion) > EXCURSION_LIMIT_LOG2, redo_exact, lambda: rest(*fast, sbw0))
```

```python
import functools

import jax
import jax.numpy as jnp
from jax import lax
from jax.experimental import pallas as pl
from jax.experimental.pallas import tpu as pltpu

F32 = jnp.float32
BF16 = jnp.bfloat16

EPS = 1e-6
LOG2E = 1.4426950408889634
CHUNK = 64
N_MOD = 6
HEADS = 4
HEAD_DIM = 128
BRANCH_W = HEADS * HEAD_DIM
DIR_W = 2 * BRANCH_W
GLA_RANK = 16
GLA_GATE_NORM = 16.0
LR_PAD = 128
G_HQ, G_HI, G_HF, G_HB, G_HGATE, G_GQ, G_GK, G_GV, G_GGATE = range(9)
LR_COL = 9 * BRANCH_W
MOD_ROWS = 16
VMEM_LIMIT_BYTES = 56 * 1024 * 1024
TAIL_ROWS = 512
TAIL_STREAMS = 2
IN_STREAMS = 2
CAST_ROW_TILE = 16
EXCURSION_LIMIT_LOG2 = 96.0


def _dot(a, b):
    return jnp.dot(a, b, preferred_element_type=F32)


def _dot_nt(a, b):
    return lax.dot_general(a, b, (((1,), (1,)), ((), ())), preferred_element_type=F32)


def _dot_tn(a, b):
    return lax.dot_general(a, b, (((0,), (0,)), ((), ())), preferred_element_type=F32)


def _sigmoid(x):
    return 1.0 / (1.0 + jnp.exp(-x))


def _silu(x):
    return x * _sigmoid(x)


def _rms(x):
    return x * lax.rsqrt(jnp.mean(x * x, axis=-1, keepdims=True) + EPS)


def _const_spec(shape):
    nd = len(shape)
    return pl.BlockSpec(shape, lambda *_: (0,) * nd, pipeline_mode=pl.Buffered(1))


def _mod_kernel(c_ref, w_ref, b_ref, o_ref):
    a = _silu(c_ref[...]).astype(BF16)
    o_ref[...] = _dot(a, w_ref[...].astype(BF16)) + b_ref[...]


def _modulation(c_all, w_mod, b_mod):
    d = c_all.shape[1]
    n = w_mod.shape[1]
    blk = 1024
    return pl.pallas_call(
        _mod_kernel,
        grid=(n // blk,),
        in_specs=[pl.BlockSpec((MOD_ROWS, d), lambda j: (0, 0)),
                  pl.BlockSpec((d, blk), lambda j: (0, j)),
                  pl.BlockSpec((1, blk), lambda j: (0, j))],
        out_specs=pl.BlockSpec((MOD_ROWS, blk), lambda j: (0, j)),
        out_shape=jax.ShapeDtypeStruct((MOD_ROWS, n), F32),
        name="modulation",
    )(c_all, w_mod, b_mod)


def _state_step(st, e_last, v, kd):
    return st * e_last + _dot_tn(v, kd)


def _inproj_kernel(x_ref, ctx_ref, mod_ref, pre_ref, w_ref, lb_ref, wgk_ref, bgk_ref, *rest,
                   d_model, rows_per_stream, n_streams, exact, n_cast):
    cast_in, rest = rest[:n_cast], rest[n_cast:]
    qdb_ref, kdb_ref, v_ref, elb_ref, oi_ref, rng_ref, sbw0_ref = rest[:7]
    cast_out, rest = rest[7:7 + n_cast], rest[7 + n_cast:]
    sfw_ref, scratch = rest[0], rest[1:]
    rs = rows_per_stream
    n_ch = rs // CHUNK
    mids = (CHUNK // 2, CHUNK // 2 - 1)
    lasts = (CHUNK - 1, 0)

    def body(z_ref, latent):
        if latent:
            for wi_ref, wo_ref in zip(cast_in, cast_out):
                wo_ref[...] = wi_ref[...].astype(BF16)
        else:
            sfw_ref[...] = jnp.zeros_like(sfw_ref)
        mod = mod_ref[0]
        shift, scale = mod[:, :d_model], mod[:, d_model:2 * d_model]

        lbr = lb_ref[...]
        e = jnp.exp(lbr - jnp.max(lbr, axis=0, keepdims=True))
        lb = e[0] / jnp.sum(e, axis=0)

        row = lax.broadcasted_iota(jnp.int32, (rs, rs), 0)
        col = lax.broadcasted_iota(jnp.int32, (rs, rs), 1)
        shift_bits = CHUNK.bit_length() - 1
        same = jnp.right_shift(row, shift_bits) == jnp.right_shift(col, shift_bits)
        masks = (same & (col <= row), same & (col >= row))
        tris = tuple(jnp.where(m, 1.0, 0.0).astype(BF16) for m in masks)

        def proj(t, g):
            return _dot(t['h'], w_ref[:, g * BRANCH_W:(g + 1) * BRANCH_W])

        span = []

        def prep(q, k, lf, d):
            lf2 = lf * LOG2E
            hi = lf2.astype(BF16)
            md = (lf2 - hi.astype(F32)).astype(BF16)
            c = _dot(tris[d], hi) + _dot(tris[d], md)
            qts, kts, qds, kds, els = [], [], [], [], []
            for n in range(n_ch):
                rows = slice(n * CHUNK, (n + 1) * CHUNK)
                cj = c[rows]
                cl = cj[lasts[d]:lasts[d] + 1]
                els.append(jnp.exp2(cl))
                if exact or not latent:
                    kds.append((k[rows] * jnp.exp2(cl - cj)).astype(BF16))
                    if latent:
                        qds.append((q[rows] * jnp.exp2(cj)).astype(BF16))
                    continue
                r = cj[mids[d]:mids[d] + 1]
                dq = cj - r
                span.append(jnp.maximum(jnp.abs(dq[0:1]), jnp.abs(dq[CHUNK - 1:CHUNK])))
                qt = q[rows] * jnp.exp2(dq)
                kt = k[rows] * jnp.exp2(-dq)
                qds.append((qt * jnp.exp2(r)).astype(BF16))
                kds.append((kt * jnp.exp2(cl - r)).astype(BF16))
                qts.append(qt.astype(BF16))
                kts.append(kt.astype(BF16))
            scores = None
            if latent:
                scores = (q, k, c) if exact else (jnp.concatenate(qts, axis=0), jnp.concatenate(kts, axis=0))
            return dict(qd=qds, kd=kds, el=els, scores=scores)

        def intra_exact(qkc, v, d):
            q, k, c = qkc
            k_s, v_s, c_s = scratch
            k_s[...] = k
            v_s[...] = v.astype(F32)
            c_s[...] = c
            rowid = lax.broadcasted_iota(jnp.int32, (CHUNK, 1), 0)
            outs = []
            for n in range(n_ch):
                base = n * CHUNK
                qj = q[base:base + CHUNK]
                cj = c[base:base + CHUNK]

                def step(s, acc):
                    cs = c_s[pl.ds(base + s, 1), :]
                    ks = k_s[pl.ds(base + s, 1), :]
                    vs = v_s[pl.ds(base + s, 1), :]
                    at_or_after = (rowid >= s) if d == 0 else (rowid <= s)
                    p = jnp.where(at_or_after, qj * jnp.exp2(jnp.minimum(cj - cs, 0.0)) * ks, 0.0)
                    parts = []
                    for hd in range(HEADS):
                        sl = slice(hd * HEAD_DIM, (hd + 1) * HEAD_DIM)
                        parts.append(jnp.sum(p[:, sl], axis=-1, keepdims=True) * vs[:, sl])
                    return acc + jnp.concatenate(parts, axis=-1)

                outs.append(lax.fori_loop(0, CHUNK, step, jnp.zeros((CHUNK, BRANCH_W), F32)))
            return jnp.concatenate(outs, axis=0)

        def intra(p, v, d):
            if not latent:
                return None
            if exact:
                return intra_exact(p['scores'], v, d)
            qt, kt = p['scores']
            outs = []
            for hd in range(HEADS):
                sl = slice(hd * HEAD_DIM, (hd + 1) * HEAD_DIM)
                a = _dot_nt(qt[:, sl], kt[:, sl])
                a = jnp.where(masks[d], a, 0.0).astype(BF16)
                outs.append(_dot(a, v[:, sl]))
            return jnp.concatenate(outs, axis=-1)

        def scan_fw(p, v, branch):
            upd = {}
            for n in range(n_ch):
                rows = slice(n * CHUNK, (n + 1) * CHUNK)
                for hd in range(HEADS):
                    sl = slice(hd * HEAD_DIM, (hd + 1) * HEAD_DIM)
                    upd[n, hd] = _dot_tn(v[rows, sl], p['kd'][n][:, sl])
            st = [sfw_ref[branch * HEADS + hd] for hd in range(HEADS)]
            outs = []
            for n in range(n_ch):
                parts = []
                for hd in range(HEADS):
                    sl = slice(hd * HEAD_DIM, (hd + 1) * HEAD_DIM)
                    if latent:
                        parts.append(_dot_nt(p['qd'][n][:, sl], st[hd].astype(BF16)))
                    st[hd] = st[hd] * p['el'][n][:, sl] + upd[n, hd]
                if latent:
                    outs.append(jnp.concatenate(parts, axis=-1))
            for hd in range(HEADS):
                sfw_ref[branch * HEADS + hd] = st[hd]
            return jnp.concatenate(outs, axis=0) if latent else None

        def emit_bw(t, p, v, branch):
            c0 = branch * BRANCH_W
            if latent:
                for n in range(n_ch):
                    rows = slice(t['first'] + n * CHUNK, t['first'] + (n + 1) * CHUNK)
                    qdb_ref[0, rows, c0:c0 + BRANCH_W] = p['qd'][n]
                    kdb_ref[0, rows, c0:c0 + BRANCH_W] = p['kd'][n]
                    elb_ref[0, t['first'] // CHUNK + n, :, c0:c0 + BRANCH_W] = p['el'][n]
                return
            for hd in range(HEADS):
                sl = slice(hd * HEAD_DIM, (hd + 1) * HEAD_DIM)
                st = jnp.zeros((HEAD_DIM, HEAD_DIM), F32)
                for n in reversed(range(n_ch)):
                    rows = slice(n * CHUNK, (n + 1) * CHUNK)
                    st = _state_step(st, p['el'][n][:, sl], v[rows, sl], p['kd'][n][:, sl])
                sbw0_ref[0, branch * HEADS + hd] = st

        def hg_gate(raw, d):
            lbd = lb[d:d + 1]
            f = lbd + (1.0 - lbd) * _sigmoid(raw)
            return 1.0 - f, jnp.log(f)

        def total(*parts):
            return functools.reduce(lambda a, b: a + b, parts).astype(BF16)

        def s_input(t):
            t['h'] = (_rms(z_ref[0, t['rows'], :]) * pre_ref[...] * (1.0 + scale) + shift).astype(BF16)

        def s_hg_proj(t):
            pq_h = proj(t, G_HQ) if latent else None
            pv_h = proj(t, G_HI)
            t['q_h'] = _silu(pq_h) if latent else None
            t['v_h'] = pv_h.astype(BF16)

        def s_hg_proj_f(t):
            t['pf'] = [proj(t, G_HF + d) for d in range(2)]

        def s_hg_fw(t):
            t['p_h0'] = prep(t['q_h'], *hg_gate(t['pf'][0], 0), 0)

        def s_gla_proj_qk(t):
            t['pq_g'] = proj(t, G_GQ) if latent else None
            t['pk_g'] = proj(t, G_GK)

        def s_gla_proj(t):
            t['pv_g'] = proj(t, G_GV)
            lr = _dot(t['h'], w_ref[:, LR_COL:LR_COL + LR_PAD]).astype(BF16)
            t['xg'] = _dot(lr, wgk_ref[...]) + bgk_ref[...]

        def s_hg_bw(t):
            t['p_h1'] = prep(t['q_h'], *hg_gate(t['pf'][1], 1), 1)
            emit_bw(t, t['p_h1'], t['v_h'], 0)

        def s_hg_intra_fw(t):
            t['o0'] = intra(t['p_h0'], t['v_h'], 0)

        def s_gla_fw(t):
            t['q_g'] = t['pq_g'] * HEAD_DIM ** -0.5 if latent else None
            t['v_g'] = t['pv_g'].astype(BF16)
            if latent:
                v_ref[0, t['rows'], 0:BRANCH_W] = t['v_h']
                v_ref[0, t['rows'], BRANCH_W:2 * BRANCH_W] = t['v_g']
            xg = t['xg']
            t['ls'] = (jnp.minimum(xg, 0.0) - jnp.log(1.0 + jnp.exp(-jnp.abs(xg)))) * (1.0 / GLA_GATE_NORM)
            t['p_g0'] = prep(t['q_g'], t['pk_g'], t['ls'][:, :BRANCH_W], 0)

        def s_hg_scan(t):
            t['os_h'] = scan_fw(t['p_h0'], t['v_h'], 0)

        def s_hg_finish(t):
            o1 = intra(t['p_h1'], t['v_h'], 1)
            if latent:
                oi_ref[0, t['rows'], 0:BRANCH_W] = total(t['o0'], o1, t['os_h'])

        def s_gla_bw(t):
            t['p_g1'] = prep(t['q_g'], t['pk_g'], t['ls'][:, BRANCH_W:], 1)
            emit_bw(t, t['p_g1'], t['v_g'], 1)

        def s_gla_intra_fw(t):
            t['og0'] = intra(t['p_g0'], t['v_g'], 0)

        def s_gla_intra_bw(t):
            t['og1'] = intra(t['p_g1'], t['v_g'], 1)

        def s_gla_finish(t):
            os0 = scan_fw(t['p_g0'], t['v_g'], 1)
            if latent:
                oi_ref[0, t['rows'], BRANCH_W:2 * BRANCH_W] = total(t['og0'], t['og1'], os0)

        n_here = n_streams if latent else 1
        streams = [dict(first=i * rs, rows=slice(i * rs, (i + 1) * rs)) for i in range(n_here)]
        for stage in (s_input, s_hg_proj, s_hg_proj_f, s_hg_fw, s_gla_proj_qk, s_gla_proj, s_hg_bw,
                      s_hg_intra_fw, s_gla_fw, s_hg_scan, s_hg_finish, s_gla_bw, s_gla_intra_fw,
                      s_gla_intra_bw, s_gla_finish):
            for t in streams:
                stage(t)

        widest = jnp.zeros((1, BRANCH_W), F32)
        for piece in span:
            widest = jnp.maximum(widest, piece)
        lanes = widest[:, 0:HEAD_DIM]
        for hd in range(1, HEADS):
            lanes = jnp.maximum(lanes, widest[:, hd * HEAD_DIM:(hd + 1) * HEAD_DIM])
        rng_ref[0, 0] = lanes

    j = pl.program_id(1)

    @pl.when(j == 0)
    def _():
        body(ctx_ref, False)

    @pl.when(j > 0)
    def _():
        body(x_ref, True)


def _in_projection(x, ctx, mod3, norm_pre, w_in_r, hg_lb, wgk, bgk, cast_weights, *, n_streams, exact):
    b, seq, d = x.shape
    rs = ctx.shape[1]
    tm = n_streams * rs
    assert rs % CHUNK == 0 and seq % tm == 0
    n_lat = seq // tm
    nw = w_in_r.shape[1]
    n_cast = len(cast_weights)
    cast_2d = [cw.reshape(cw.shape[-2:]) for cw in cast_weights]
    kern = functools.partial(_inproj_kernel, d_model=d, rows_per_stream=rs, n_streams=n_streams,
                             exact=exact, n_cast=n_cast)

    def cast_spec(cw):
        tiles = cw.shape[0] // CAST_ROW_TILE
        n_slabs = max(n for n in range(1, b * n_lat + 1) if tiles % n == 0)
        return pl.BlockSpec(
            (cw.shape[0] // n_slabs, cw.shape[1]),
            lambda i, j: (jnp.minimum(i * n_lat + jnp.maximum(j - 1, 0), n_slabs - 1), 0))

    def latent_rows(w):
        return pl.BlockSpec((1, tm, w), lambda i, j: (i, jnp.maximum(j - 1, 0), 0))

    n_state = 2 * HEADS
    return pl.pallas_call(
        kern,
        grid=(b, n_lat + 1),
        in_specs=[
            latent_rows(d),
            pl.BlockSpec((1, rs, d), lambda i, j: (i, 0, 0)),
            pl.BlockSpec((1, 1, N_MOD * d), lambda i, j: (jnp.where(j == 0, 0, i + 1), 0, 0)),
            _const_spec((1, d)),
            _const_spec((d, nw)),
            _const_spec(hg_lb.shape),
            _const_spec(wgk.shape),
            _const_spec(bgk.shape),
        ] + [cast_spec(cw) for cw in cast_2d],
        out_specs=[
            latent_rows(DIR_W),
            latent_rows(DIR_W),
            latent_rows(DIR_W),
            pl.BlockSpec((1, tm // CHUNK, 1, DIR_W), lambda i, j: (i, jnp.maximum(j - 1, 0), 0, 0)),
            latent_rows(DIR_W),
            pl.BlockSpec((1, 1, 1, HEAD_DIM), lambda i, j: (i, j, 0, 0)),
            pl.BlockSpec((1, n_state, HEAD_DIM, HEAD_DIM), lambda i, j: (i, 0, 0, 0)),
        ] + [cast_spec(cw) for cw in cast_2d],
        out_shape=[
            jax.ShapeDtypeStruct((b, seq, DIR_W), BF16),
            jax.ShapeDtypeStruct((b, seq, DIR_W), BF16),
            jax.ShapeDtypeStruct((b, seq, DIR_W), BF16),
            jax.ShapeDtypeStruct((b, seq // CHUNK, 1, DIR_W), F32),
            jax.ShapeDtypeStruct((b, seq, DIR_W), BF16),
            jax.ShapeDtypeStruct((b, n_lat + 1, 1, HEAD_DIM), F32),
            jax.ShapeDtypeStruct((b, n_state, HEAD_DIM, HEAD_DIM), F32),
        ] + [jax.ShapeDtypeStruct(cw.shape, BF16) for cw in cast_2d],
        scratch_shapes=([pltpu.VMEM((n_state, HEAD_DIM, HEAD_DIM), F32)]
                        + ([pltpu.VMEM((rs, BRANCH_W), F32)] * 3 if exact else [])),
        compiler_params=pltpu.CompilerParams(
            dimension_semantics=("arbitrary", "arbitrary"),
            vmem_limit_bytes=VMEM_LIMIT_BYTES),
        name="in_projection_exact" if exact else "in_projection",
    )(x, ctx, mod3, norm_pre, w_in_r, hg_lb, wgk, bgk, *cast_2d)


def _tail_kernel(oi_ref, qdb_ref, kdb_ref, v_ref, elb_ref, sbw0_ref, x_ref, mod_ref,
                 hgn_ref, gln_ref, pre1_ref, post1_ref, pre2_ref, post2_ref,
                 wgh_ref, wgg_ref, wgm_ref, wbh_ref, wbg_ref, wout_ref, wg_ref, wu_ref, wd_ref,
                 out_ref, sbw_ref, *, d_model, n_streams):
    mod = mod_ref[0]
    m = [mod[:, i * d_model:(i + 1) * d_model] for i in range(N_MOD)]
    rows_per_stream = x_ref.shape[1] // n_streams
    chunks_per_stream = rows_per_stream // CHUNK
    gates = (wgh_ref, wgg_ref)
    norms = (hgn_ref, gln_ref)
    branch_w = (wbh_ref, wbg_ref)
    streams = [dict(first=i * rows_per_stream, rows=slice(i * rows_per_stream, (i + 1) * rows_per_stream))
               for i in range(n_streams)]

    @pl.when(pl.program_id(1) == 0)
    def _():
        sbw_ref[...] = sbw0_ref[0]

    def scan_bw(t):
        parts = [None] * chunks_per_stream
        for n in reversed(range(chunks_per_stream)):
            rows = slice(t['first'] + n * CHUNK, t['first'] + (n + 1) * CHUNK)
            e_last = elb_ref[0, t['first'] // CHUNK + n]
            heads = []
            for u in range(2 * HEADS):
                sl = slice(u * HEAD_DIM, (u + 1) * HEAD_DIM)
                st = sbw_ref[u]
                heads.append(_dot_nt(qdb_ref[0, rows, sl], st.astype(BF16)))
                sbw_ref[u] = _state_step(st, e_last[:, sl], v_ref[0, rows, sl], kdb_ref[0, rows, sl])
            parts[n] = jnp.concatenate(heads, axis=-1)
        t['o_bw'] = jnp.concatenate(parts, axis=0)

    def mixer_input(t):
        t['x'] = x_ref[0, t['rows'], :]
        t['h1'] = (_rms(t['x']) * pre1_ref[...] * (1.0 + m[1]) + m[0]).astype(BF16)

    def gate_proj(t):
        t['out_gate'] = [_silu(_dot(t['h1'], gates[br][...])) for br in range(2)]
        t['merge_gate'] = _sigmoid(_dot(t['h1'], wgm_ref[...]))

    def branch_proj(t):
        o = oi_ref[0, t['rows'], :].astype(F32) + t['o_bw']
        t['ys'] = []
        for br in range(2):
            parts = []
            for hd in range(HEADS):
                c0 = br * BRANCH_W + hd * HEAD_DIM
                parts.append(_rms(o[:, c0:c0 + HEAD_DIM]) * norms[br][...])
            og = (jnp.concatenate(parts, axis=-1) * t['out_gate'][br]).astype(BF16)
            t['ys'].append(_dot(og, branch_w[br][...]))

    def out_proj(t):
        gm = t['merge_gate']
        merged = gm[:, :d_model] * t['ys'][0] + gm[:, d_model:] * t['ys'][1]
        t['y'] = _dot(merged.astype(BF16), wout_ref[...])

    def ffn_input(t):
        t['z1'] = t['x'] + _rms(t['y']) * post1_ref[...] * m[2]
        t['h2'] = (_rms(t['z1']) * pre2_ref[...] * (1.0 + m[4]) + m[3]).astype(BF16)

    def ffn_up(t):
        g = _dot(t['h2'], wg_ref[...])
        u = _dot(t['h2'], wu_ref[...])
        t['a'] = (_silu(g) * u).astype(BF16)

    def ffn_down(t):
        t['y2'] = _dot(t['a'], wd_ref[...])

    def residual_out(t):
        out_ref[0, t['rows'], :] = t['z1'] + _rms(t['y2']) * post2_ref[...] * m[5]

    for t in reversed(streams):
        scan_bw(t)
    for stage in (mixer_input, gate_proj, branch_proj, out_proj, ffn_input, ffn_up, ffn_down, residual_out):
        for t in streams:
            stage(t)


def _tail(oi, qdb, kdb, v, elb, sbw0, x, mod3, *consts, tm):
    b, seq, d = x.shape
    n_tiles = seq // tm
    kern = functools.partial(_tail_kernel, d_model=d, n_streams=TAIL_STREAMS)

    def rows(w):
        return pl.BlockSpec((1, tm, w), lambda i, j: (i, n_tiles - 1 - j, 0))

    return pl.pallas_call(
        kern,
        grid=(b, n_tiles),
        in_specs=[
            rows(DIR_W), rows(DIR_W), rows(DIR_W), rows(DIR_W),
            pl.BlockSpec((1, tm // CHUNK, 1, DIR_W), lambda i, j: (i, n_tiles - 1 - j, 0, 0)),
            pl.BlockSpec((1,) + sbw0.shape[1:], lambda i, j: (i, 0, 0, 0)),
            rows(d),
            pl.BlockSpec((1, 1, N_MOD * d), lambda i, j: (i + 1, 0, 0)),
        ] + [_const_spec(c.shape) for c in consts],
        out_specs=rows(d),
        out_shape=jax.ShapeDtypeStruct((b, seq, d), F32),
        scratch_shapes=[pltpu.VMEM(sbw0.shape[1:], F32)],
        compiler_params=pltpu.CompilerParams(
            dimension_semantics=("arbitrary", "arbitrary"),
            vmem_limit_bytes=VMEM_LIMIT_BYTES),
        name="mixer_out_ffn",
    )(oi, qdb, kdb, v, elb, sbw0, x, mod3, *consts)


def kernel(x, c, ctx, c_ctx, w_mod, b_mod, norm_pre1, norm_post1, norm_pre2, norm_post2, w_in, hg_lb,
           hg_onorm, gla_w_gk, gla_b_gk, gla_onorm, w_br_hg, w_br_gla, w_out, w_ff_gate, w_ff_up, w_ff_down):
    b, seq, d = x.shape
    ctx_len = ctx.shape[1]
    assert w_mod.shape[0] == 1 and d == DIR_W
    assert seq % CHUNK == 0 and ctx_len % CHUNK == 0

    c_all = jnp.zeros((MOD_ROWS, d), F32).at[0].set(c_ctx).at[1:b + 1].set(c)
    mod3 = _modulation(c_all, w_mod.reshape(d, N_MOD * d), b_mod).reshape(MOD_ROWS, 1, N_MOD * d)

    def bf(a):
        return a.reshape(a.shape[1:]).astype(BF16)

    w = bf(w_in)
    lr_w = 2 * GLA_RANK
    wgk = jnp.zeros((LR_PAD, 2 * BRANCH_W), F32)
    wgk = wgk.at[:GLA_RANK, :BRANCH_W].set(gla_w_gk[0, 0]).at[GLA_RANK:lr_w, BRANCH_W:].set(gla_w_gk[0, 1])
    bgk = gla_b_gk.reshape(1, 2 * BRANCH_W)

    tail_weights = (w_br_hg, w_br_gla, w_out, w_ff_gate, w_ff_up, w_ff_down)

    def in_projection(exact):
        return _in_projection(x, ctx, mod3, norm_pre1, w, hg_lb, wgk.astype(BF16), bgk,
                              () if exact else tail_weights, n_streams=IN_STREAMS, exact=exact)

    *fast, excursion, sbw0, wbh, wbg, wo, wfg, wfu, wfd = in_projection(False)
    tail_bf16 = (wbh, wbg, wo, wfg, wfu, wfd)

    def rest(qdb, kdb, v, elb, oi, sbw0):
        return _tail(oi, qdb, kdb, v, elb, sbw0, x, mod3,
                     hg_onorm, gla_onorm, norm_pre1, norm_post1, norm_pre2, norm_post2,
                     w[:, G_HGATE * BRANCH_W:(G_HGATE + 1) * BRANCH_W],
                     w[:, G_GGATE * BRANCH_W:(G_GGATE + 1) * BRANCH_W],
                     w[:, LR_COL + lr_w:],
                     *tail_bf16, tm=TAIL_ROWS)

    def redo_exact():
        qdb, kdb, v, elb, oi, _, sbw0_exact = in_projection(True)[:7]
        return rest(qdb, kdb, v, elb, oi, sbw0_exact)

    return lax.cond(jnp.max(excursion) > EXCURSION_LIMIT_LOG2, redo_exact, lambda: rest(*fast, sbw0))
```

```python
import functools

import jax
import jax.numpy as jnp
from jax import lax
from jax.experimental import pallas as pl
from jax.experimental.pallas import tpu as pltpu

F32 = jnp.float32
BF16 = jnp.bfloat16

EPS = 1e-6
LOG2E = 1.4426950408889634
CHUNK = 64
N_MOD = 6
HEADS = 4
HEAD_DIM = 128
BRANCH_W = HEADS * HEAD_DIM
DIR_W = 2 * BRANCH_W
GLA_RANK = 16
GLA_GATE_NORM = 16.0
LR_PAD = 128
G_HQ, G_HI, G_HF, G_HB, G_HGATE, G_GQ, G_GK, G_GV, G_GGATE = range(9)
LR_COL = 9 * BRANCH_W
GATE_COLS = ((G_HGATE * BRANCH_W, (G_HGATE + 1) * BRANCH_W), (G_GGATE * BRANCH_W, (G_GGATE + 1) * BRANCH_W),
             (LR_COL + 2 * GLA_RANK, LR_COL + 2 * GLA_RANK + 4 * BRANCH_W))
MOD_ROWS = 16
VMEM_LIMIT_BYTES = 56 * 1024 * 1024
TAIL_ROWS = 512
TAIL_STREAMS = 2
IN_STREAMS = 2
CAST_ROW_TILE = 16
EXCURSION_LIMIT_LOG2 = 96.0


def _dot(a, b):
    return jnp.dot(a, b, preferred_element_type=F32)


def _dot_nt(a, b):
    return lax.dot_general(a, b, (((1,), (1,)), ((), ())), preferred_element_type=F32)


def _dot_tn(a, b):
    return lax.dot_general(a, b, (((0,), (0,)), ((), ())), preferred_element_type=F32)


def _sigmoid(x):
    return 1.0 / (1.0 + jnp.exp(-x))


def _silu(x):
    return x * _sigmoid(x)


def _rms(x):
    return x * lax.rsqrt(jnp.mean(x * x, axis=-1, keepdims=True) + EPS)


def _const_spec(shape):
    nd = len(shape)
    return pl.BlockSpec(shape, lambda *_: (0,) * nd, pipeline_mode=pl.Buffered(1))


def _mod_kernel(c_ref, w_ref, b_ref, o_ref):
    a = _silu(c_ref[...]).astype(BF16)
    o_ref[...] = _dot(a, w_ref[...].astype(BF16)) + b_ref[...]


def _modulation(c_all, w_mod, b_mod):
    d = c_all.shape[1]
    n = w_mod.shape[1]
    blk = 1024
    return pl.pallas_call(
        _mod_kernel,
        grid=(n // blk,),
        in_specs=[pl.BlockSpec((MOD_ROWS, d), lambda j: (0, 0)),
                  pl.BlockSpec((d, blk), lambda j: (0, j)),
                  pl.BlockSpec((1, blk), lambda j: (0, j))],
        out_specs=pl.BlockSpec((MOD_ROWS, blk), lambda j: (0, j)),
        out_shape=jax.ShapeDtypeStruct((MOD_ROWS, n), F32),
        name="modulation",
    )(c_all, w_mod, b_mod)


def _state_step(st, e_last, v, kd):
    return st * e_last + _dot_tn(v, kd)


def _inproj_kernel(x_ref, ctx_ref, mod_ref, pre_ref, w_ref, lb_ref, wgk_ref, bgk_ref, *rest,
                   d_model, rows_per_stream, n_streams, exact, n_cast, n_gate):
    cast_in, rest = rest[:n_cast], rest[n_cast:]
    qdb_ref, kdb_ref, v_ref, elb_ref, oi_ref, rng_ref, sbw0_ref = rest[:7]
    cast_out, rest = rest[7:7 + n_cast], rest[7 + n_cast:]
    gate_out, rest = rest[:n_gate], rest[n_gate:]
    sfw_ref, scratch = rest[0], rest[1:]
    rs = rows_per_stream
    n_ch = rs // CHUNK
    mids = (CHUNK // 2, CHUNK // 2 - 1)
    lasts = (CHUNK - 1, 0)

    def body(z_ref, latent):
        if latent:
            for wi_ref, wo_ref in zip(cast_in, cast_out):
                wo_ref[...] = wi_ref[...].astype(BF16)
            if n_gate:
                slab = gate_out[0].shape[0]
                step = pl.program_id(0) * (pl.num_programs(1) - 1) + pl.program_id(1) - 1
                r0 = pl.multiple_of(step * slab, slab)
                for (c0, c1), go_ref in zip(GATE_COLS, gate_out):
                    go_ref[...] = w_ref[pl.ds(r0, slab), c0:c1]
        else:
            sfw_ref[...] = jnp.zeros_like(sfw_ref)
        mod = mod_ref[0]
        shift, scale = mod[:, :d_model], mod[:, d_model:2 * d_model]

        lbr = lb_ref[...]
        e = jnp.exp(lbr - jnp.max(lbr, axis=0, keepdims=True))
        lb = e[0] / jnp.sum(e, axis=0)

        row = lax.broadcasted_iota(jnp.int32, (rs, rs), 0)
        col = lax.broadcasted_iota(jnp.int32, (rs, rs), 1)
        shift_bits = CHUNK.bit_length() - 1
        same = jnp.right_shift(row, shift_bits) == jnp.right_shift(col, shift_bits)
        masks = (same & (col <= row), same & (col >= row))
        tris = tuple(jnp.where(m, 1.0, 0.0).astype(BF16) for m in masks)

        def proj(t, g):
            return _dot(t['h'], w_ref[:, g * BRANCH_W:(g + 1) * BRANCH_W])

        span = []

        def prep(q, k, lf, d):
            lf2 = lf * LOG2E
            hi = lf2.astype(BF16)
            md = (lf2 - hi.astype(F32)).astype(BF16)
            c = _dot(tris[d], hi) + _dot(tris[d], md)
            qts, kts, qds, kds, els = [], [], [], [], []
            for n in range(n_ch):
                rows = slice(n * CHUNK, (n + 1) * CHUNK)
                cj = c[rows]
                cl = cj[lasts[d]:lasts[d] + 1]
                els.append(jnp.exp2(cl))
                if exact or not latent:
                    kds.append((k[rows] * jnp.exp2(cl - cj)).astype(BF16))
                    if latent:
                        qds.append((q[rows] * jnp.exp2(cj)).astype(BF16))
                    continue
                r = cj[mids[d]:mids[d] + 1]
                dq = cj - r
                span.append(jnp.maximum(jnp.abs(dq[0:1]), jnp.abs(dq[CHUNK - 1:CHUNK])))
                qt = q[rows] * jnp.exp2(dq)
                kt = k[rows] * jnp.exp2(-dq)
                qds.append((qt * jnp.exp2(r)).astype(BF16))
                kds.append((kt * jnp.exp2(cl - r)).astype(BF16))
                qts.append(qt.astype(BF16))
                kts.append(kt.astype(BF16))
            scores = None
            if latent:
                scores = (q, k, c) if exact else (jnp.concatenate(qts, axis=0), jnp.concatenate(kts, axis=0))
            return dict(qd=qds, kd=kds, el=els, scores=scores)

        def intra_exact(qkc, v, d):
            q, k, c = qkc
            k_s, v_s, c_s = scratch
            k_s[...] = k
            v_s[...] = v.astype(F32)
            c_s[...] = c
            rowid = lax.broadcasted_iota(jnp.int32, (CHUNK, 1), 0)
            outs = []
            for n in range(n_ch):
                base = n * CHUNK
                qj = q[base:base + CHUNK]
                cj = c[base:base + CHUNK]

                def step(s, acc):
                    cs = c_s[pl.ds(base + s, 1), :]
                    ks = k_s[pl.ds(base + s, 1), :]
                    vs = v_s[pl.ds(base + s, 1), :]
                    at_or_after = (rowid >= s) if d == 0 else (rowid <= s)
                    p = jnp.where(at_or_after, qj * jnp.exp2(jnp.minimum(cj - cs, 0.0)) * ks, 0.0)
                    parts = []
                    for hd in range(HEADS):
                        sl = slice(hd * HEAD_DIM, (hd + 1) * HEAD_DIM)
                        parts.append(jnp.sum(p[:, sl], axis=-1, keepdims=True) * vs[:, sl])
                    return acc + jnp.concatenate(parts, axis=-1)

                outs.append(lax.fori_loop(0, CHUNK, step, jnp.zeros((CHUNK, BRANCH_W), F32)))
            return jnp.concatenate(outs, axis=0)

        def intra(p, v, d):
            if not latent:
                return None
            if exact:
                return intra_exact(p['scores'], v, d)
            qt, kt = p['scores']
            outs = []
            for hd in range(HEADS):
                sl = slice(hd * HEAD_DIM, (hd + 1) * HEAD_DIM)
                a = _dot_nt(qt[:, sl], kt[:, sl])
                a = jnp.where(masks[d], a, 0.0).astype(BF16)
                outs.append(_dot(a, v[:, sl]))
            return jnp.concatenate(outs, axis=-1)

        def scan_fw(p, v, branch):
            upd = {}
            for n in range(n_ch):
                rows = slice(n * CHUNK, (n + 1) * CHUNK)
                for hd in range(HEADS):
                    sl = slice(hd * HEAD_DIM, (hd + 1) * HEAD_DIM)
                    upd[n, hd] = _dot_tn(v[rows, sl], p['kd'][n][:, sl])
            st = [sfw_ref[branch * HEADS + hd] for hd in range(HEADS)]
            outs = []
            for n in range(n_ch):
                parts = []
                for hd in range(HEADS):
                    sl = slice(hd * HEAD_DIM, (hd + 1) * HEAD_DIM)
                    if latent:
                        parts.append(_dot_nt(p['qd'][n][:, sl], st[hd].astype(BF16)))
                    st[hd] = st[hd] * p['el'][n][:, sl] + upd[n, hd]
                if latent:
                    outs.append(jnp.concatenate(parts, axis=-1))
            for hd in range(HEADS):
                sfw_ref[branch * HEADS + hd] = st[hd]
            return jnp.concatenate(outs, axis=0) if latent else None

        def emit_bw(t, p, v, branch):
            c0 = branch * BRANCH_W
            if latent:
                for n in range(n_ch):
                    rows = slice(t['first'] + n * CHUNK, t['first'] + (n + 1) * CHUNK)
                    qdb_ref[0, rows, c0:c0 + BRANCH_W] = p['qd'][n]
                    kdb_ref[0, rows, c0:c0 + BRANCH_W] = p['kd'][n]
                    elb_ref[0, t['first'] // CHUNK + n, :, c0:c0 + BRANCH_W] = p['el'][n]
                return
            for hd in range(HEADS):
                sl = slice(hd * HEAD_DIM, (hd + 1) * HEAD_DIM)
                st = jnp.zeros((HEAD_DIM, HEAD_DIM), F32)
                for n in reversed(range(n_ch)):
                    rows = slice(n * CHUNK, (n + 1) * CHUNK)
                    st = _state_step(st, p['el'][n][:, sl], v[rows, sl], p['kd'][n][:, sl])
                sbw0_ref[0, branch * HEADS + hd] = st

        def hg_gate(raw, d):
            lbd = lb[d:d + 1]
            f = lbd + (1.0 - lbd) * _sigmoid(raw)
            return 1.0 - f, jnp.log(f)

        def total(*parts):
            return functools.reduce(lambda a, b: a + b, parts).astype(BF16)

        def s_input(t):
            t['h'] = (_rms(z_ref[0, t['rows'], :]) * pre_ref[...] * (1.0 + scale) + shift).astype(BF16)

        def s_hg_proj(t):
            pq_h = proj(t, G_HQ) if latent else None
            pv_h = proj(t, G_HI)
            t['q_h'] = _silu(pq_h) if latent else None
            t['v_h'] = pv_h.astype(BF16)

        def s_hg_proj_f(t):
            t['pf'] = [proj(t, G_HF + d) for d in range(2)]

        def s_hg_fw(t):
            t['p_h0'] = prep(t['q_h'], *hg_gate(t['pf'][0], 0), 0)

        def s_gla_proj_qk(t):
            t['pq_g'] = proj(t, G_GQ) if latent else None
            t['pk_g'] = proj(t, G_GK)

        def s_gla_proj(t):
            t['pv_g'] = proj(t, G_GV)
            lr = _dot(t['h'], w_ref[:, LR_COL:LR_COL + LR_PAD]).astype(BF16)
            t['xg'] = _dot(lr, wgk_ref[...]) + bgk_ref[...]

        def s_hg_bw(t):
            t['p_h1'] = prep(t['q_h'], *hg_gate(t['pf'][1], 1), 1)
            emit_bw(t, t['p_h1'], t['v_h'], 0)

        def s_hg_intra_fw(t):
            t['o0'] = intra(t['p_h0'], t['v_h'], 0)

        def s_gla_fw(t):
            t['q_g'] = t['pq_g'] * HEAD_DIM ** -0.5 if latent else None
            t['v_g'] = t['pv_g'].astype(BF16)
            if latent:
                v_ref[0, t['rows'], 0:BRANCH_W] = t['v_h']
                v_ref[0, t['rows'], BRANCH_W:2 * BRANCH_W] = t['v_g']
            xg = t['xg']
            t['ls'] = (jnp.minimum(xg, 0.0) - jnp.log(1.0 + jnp.exp(-jnp.abs(xg)))) * (1.0 / GLA_GATE_NORM)
            t['p_g0'] = prep(t['q_g'], t['pk_g'], t['ls'][:, :BRANCH_W], 0)

        def s_hg_scan(t):
            t['os_h'] = scan_fw(t['p_h0'], t['v_h'], 0)

        def s_hg_finish(t):
            o1 = intra(t['p_h1'], t['v_h'], 1)
            if latent:
                oi_ref[0, t['rows'], 0:BRANCH_W] = total(t['o0'], o1, t['os_h'])

        def s_gla_bw(t):
            t['p_g1'] = prep(t['q_g'], t['pk_g'], t['ls'][:, BRANCH_W:], 1)
            emit_bw(t, t['p_g1'], t['v_g'], 1)

        def s_gla_intra_fw(t):
            t['og0'] = intra(t['p_g0'], t['v_g'], 0)

        def s_gla_intra_bw(t):
            t['og1'] = intra(t['p_g1'], t['v_g'], 1)

        def s_gla_finish(t):
            os0 = scan_fw(t['p_g0'], t['v_g'], 1)
            if latent:
                oi_ref[0, t['rows'], BRANCH_W:2 * BRANCH_W] = total(t['og0'], t['og1'], os0)

        n_here = n_streams if latent else 1
        streams = [dict(first=i * rs, rows=slice(i * rs, (i + 1) * rs)) for i in range(n_here)]
        for stage in (s_input, s_hg_proj, s_hg_proj_f, s_hg_fw, s_gla_proj_qk, s_gla_proj, s_hg_bw,
                      s_hg_intra_fw, s_gla_fw, s_hg_scan, s_hg_finish, s_gla_bw, s_gla_intra_fw,
                      s_gla_intra_bw, s_gla_finish):
            for t in streams:
                stage(t)

        widest = jnp.zeros((1, BRANCH_W), F32)
        for piece in span:
            widest = jnp.maximum(widest, piece)
        lanes = widest[:, 0:HEAD_DIM]
        for hd in range(1, HEADS):
            lanes = jnp.maximum(lanes, widest[:, hd * HEAD_DIM:(hd + 1) * HEAD_DIM])
        rng_ref[0, 0] = lanes

    j = pl.program_id(1)

    @pl.when(j == 0)
    def _():
        body(ctx_ref, False)

    @pl.when(j > 0)
    def _():
        body(x_ref, True)


def _in_projection(x, ctx, mod3, norm_pre, w_in_r, hg_lb, wgk, bgk, cast_weights, *, n_streams, exact):
    b, seq, d = x.shape
    rs = ctx.shape[1]
    tm = n_streams * rs
    assert rs % CHUNK == 0 and seq % tm == 0
    n_lat = seq // tm
    nw = w_in_r.shape[1]
    n_cast = len(cast_weights)
    cast_2d = [cw.reshape(cw.shape[-2:]) for cw in cast_weights]
    n_gate = 0 if exact else len(GATE_COLS)
    steps = b * n_lat
    assert d % (steps * CAST_ROW_TILE) == 0
    gate_specs = [pl.BlockSpec((d // steps, c1 - c0), lambda i, j: (i * n_lat + jnp.maximum(j - 1, 0), 0))
                  for c0, c1 in GATE_COLS[:n_gate]]
    gate_shapes = [jax.ShapeDtypeStruct((d, c1 - c0), BF16) for c0, c1 in GATE_COLS[:n_gate]]
    kern = functools.partial(_inproj_kernel, d_model=d, rows_per_stream=rs, n_streams=n_streams,
                             exact=exact, n_cast=n_cast, n_gate=n_gate)

    def cast_spec(cw):
        tiles = cw.shape[0] // CAST_ROW_TILE
        n_slabs = max(n for n in range(1, b * n_lat + 1) if tiles % n == 0)
        return pl.BlockSpec(
            (cw.shape[0] // n_slabs, cw.shape[1]),
            lambda i, j: (jnp.minimum(i * n_lat + jnp.maximum(j - 1, 0), n_slabs - 1), 0))

    def latent_rows(w):
        return pl.BlockSpec((1, tm, w), lambda i, j: (i, jnp.maximum(j - 1, 0), 0))

    n_state = 2 * HEADS
    return pl.pallas_call(
        kern,
        grid=(b, n_lat + 1),
        in_specs=[
            latent_rows(d),
            pl.BlockSpec((1, rs, d), lambda i, j: (i, 0, 0)),
            pl.BlockSpec((1, 1, N_MOD * d), lambda i, j: (jnp.where(j == 0, 0, i + 1), 0, 0)),
            _const_spec((1, d)),
            _const_spec((d, nw)),
            _const_spec(hg_lb.shape),
            _const_spec(wgk.shape),
            _const_spec(bgk.shape),
        ] + [cast_spec(cw) for cw in cast_2d],
        out_specs=[
            latent_rows(DIR_W),
            latent_rows(DIR_W),
            latent_rows(DIR_W),
            pl.BlockSpec((1, tm // CHUNK, 1, DIR_W), lambda i, j: (i, jnp.maximum(j - 1, 0), 0, 0)),
            latent_rows(DIR_W),
            pl.BlockSpec((1, 1, 1, HEAD_DIM), lambda i, j: (i, j, 0, 0)),
            pl.BlockSpec((1, n_state, HEAD_DIM, HEAD_DIM), lambda i, j: (i, 0, 0, 0)),
        ] + [cast_spec(cw) for cw in cast_2d] + gate_specs,
        out_shape=[
            jax.ShapeDtypeStruct((b, seq, DIR_W), BF16),
            jax.ShapeDtypeStruct((b, seq, DIR_W), BF16),
            jax.ShapeDtypeStruct((b, seq, DIR_W), BF16),
            jax.ShapeDtypeStruct((b, seq // CHUNK, 1, DIR_W), F32),
            jax.ShapeDtypeStruct((b, seq, DIR_W), BF16),
            jax.ShapeDtypeStruct((b, n_lat + 1, 1, HEAD_DIM), F32),
            jax.ShapeDtypeStruct((b, n_state, HEAD_DIM, HEAD_DIM), F32),
        ] + [jax.ShapeDtypeStruct(cw.shape, BF16) for cw in cast_2d] + gate_shapes,
        scratch_shapes=([pltpu.VMEM((n_state, HEAD_DIM, HEAD_DIM), F32)]
                        + ([pltpu.VMEM((rs, BRANCH_W), F32)] * 3 if exact else [])),
        compiler_params=pltpu.CompilerParams(
            dimension_semantics=("arbitrary", "arbitrary"),
            vmem_limit_bytes=VMEM_LIMIT_BYTES),
        name="in_projection_exact" if exact else "in_projection",
    )(x, ctx, mod3, norm_pre, w_in_r, hg_lb, wgk, bgk, *cast_2d)


def _tail_kernel(oi_ref, qdb_ref, kdb_ref, v_ref, elb_ref, sbw0_ref, x_ref, mod_ref,
                 hgn_ref, gln_ref, pre1_ref, post1_ref, pre2_ref, post2_ref,
                 wgh_ref, wgg_ref, wgm_ref, wbh_ref, wbg_ref, wout_ref, wg_ref, wu_ref, wd_ref,
                 out_ref, sbw_ref, *, d_model, n_streams):
    mod = mod_ref[0]
    m = [mod[:, i * d_model:(i + 1) * d_model] for i in range(N_MOD)]
    rows_per_stream = x_ref.shape[1] // n_streams
    chunks_per_stream = rows_per_stream // CHUNK
    gates = (wgh_ref, wgg_ref)
    norms = (hgn_ref, gln_ref)
    branch_w = (wbh_ref, wbg_ref)
    streams = [dict(first=i * rows_per_stream, rows=slice(i * rows_per_stream, (i + 1) * rows_per_stream))
               for i in range(n_streams)]

    @pl.when(pl.program_id(1) == 0)
    def _():
        sbw_ref[...] = sbw0_ref[0]

    def scan_bw(t):
        parts = [None] * chunks_per_stream
        for n in reversed(range(chunks_per_stream)):
            rows = slice(t['first'] + n * CHUNK, t['first'] + (n + 1) * CHUNK)
            e_last = elb_ref[0, t['first'] // CHUNK + n]
            heads = []
            for u in range(2 * HEADS):
                sl = slice(u * HEAD_DIM, (u + 1) * HEAD_DIM)
                st = sbw_ref[u]
                heads.append(_dot_nt(qdb_ref[0, rows, sl], st.astype(BF16)))
                sbw_ref[u] = _state_step(st, e_last[:, sl], v_ref[0, rows, sl], kdb_ref[0, rows, sl])
            parts[n] = jnp.concatenate(heads, axis=-1)
        t['o_bw'] = jnp.concatenate(parts, axis=0)

    def mixer_input(t):
        t['x'] = x_ref[0, t['rows'], :]
        t['h1'] = (_rms(t['x']) * pre1_ref[...] * (1.0 + m[1]) + m[0]).astype(BF16)

    def gate_proj(t):
        t['out_gate'] = [_silu(_dot(t['h1'], gates[br][...])) for br in range(2)]
        t['merge_gate'] = _sigmoid(_dot(t['h1'], wgm_ref[...]))

    def branch_proj(t):
        o = oi_ref[0, t['rows'], :].astype(F32) + t['o_bw']
        t['ys'] = []
        for br in range(2):
            parts = []
            for hd in range(HEADS):
                c0 = br * BRANCH_W + hd * HEAD_DIM
                parts.append(_rms(o[:, c0:c0 + HEAD_DIM]) * norms[br][...])
            og = (jnp.concatenate(parts, axis=-1) * t['out_gate'][br]).astype(BF16)
            t['ys'].append(_dot(og, branch_w[br][...]))

    def out_proj(t):
        gm = t['merge_gate']
        merged = gm[:, :d_model] * t['ys'][0] + gm[:, d_model:] * t['ys'][1]
        t['y'] = _dot(merged.astype(BF16), wout_ref[...])

    def ffn_input(t):
        t['z1'] = t['x'] + _rms(t['y']) * post1_ref[...] * m[2]
        t['h2'] = (_rms(t['z1']) * pre2_ref[...] * (1.0 + m[4]) + m[3]).astype(BF16)

    def ffn_up(t):
        g = _dot(t['h2'], wg_ref[...])
        u = _dot(t['h2'], wu_ref[...])
        t['a'] = (_silu(g) * u).astype(BF16)

    def ffn_down(t):
        t['y2'] = _dot(t['a'], wd_ref[...])

    def residual_out(t):
        out_ref[0, t['rows'], :] = t['z1'] + _rms(t['y2']) * post2_ref[...] * m[5]

    for t in reversed(streams):
        scan_bw(t)
    for stage in (mixer_input, gate_proj, branch_proj, out_proj, ffn_input, ffn_up, ffn_down, residual_out):
        for t in streams:
            stage(t)


def _tail(oi, qdb, kdb, v, elb, sbw0, x, mod3, *consts, tm):
    b, seq, d = x.shape
    n_tiles = seq // tm
    kern = functools.partial(_tail_kernel, d_model=d, n_streams=TAIL_STREAMS)

    def rows(w):
        return pl.BlockSpec((1, tm, w), lambda i, j: (i, n_tiles - 1 - j, 0))

    return pl.pallas_call(
        kern,
        grid=(b, n_tiles),
        in_specs=[
            rows(DIR_W), rows(DIR_W), rows(DIR_W), rows(DIR_W),
            pl.BlockSpec((1, tm // CHUNK, 1, DIR_W), lambda i, j: (i, n_tiles - 1 - j, 0, 0)),
            pl.BlockSpec((1,) + sbw0.shape[1:], lambda i, j: (i, 0, 0, 0)),
            rows(d),
            pl.BlockSpec((1, 1, N_MOD * d), lambda i, j: (i + 1, 0, 0)),
        ] + [_const_spec(c.shape) for c in consts],
        out_specs=rows(d),
        out_shape=jax.ShapeDtypeStruct((b, seq, d), F32),
        scratch_shapes=[pltpu.VMEM(sbw0.shape[1:], F32)],
        compiler_params=pltpu.CompilerParams(
            dimension_semantics=("arbitrary", "arbitrary"),
            vmem_limit_bytes=VMEM_LIMIT_BYTES),
        name="mixer_out_ffn",
    )(oi, qdb, kdb, v, elb, sbw0, x, mod3, *consts)


def kernel(x, c, ctx, c_ctx, w_mod, b_mod, norm_pre1, norm_post1, norm_pre2, norm_post2, w_in, hg_lb,
           hg_onorm, gla_w_gk, gla_b_gk, gla_onorm, w_br_hg, w_br_gla, w_out, w_ff_gate, w_ff_up, w_ff_down):
    b, seq, d = x.shape
    ctx_len = ctx.shape[1]
    assert w_mod.shape[0] == 1 and d == DIR_W
    assert seq % CHUNK == 0 and ctx_len % CHUNK == 0

    c_all = jnp.zeros((MOD_ROWS, d), F32).at[0].set(c_ctx).at[1:b + 1].set(c)
    mod3 = _modulation(c_all, w_mod.reshape(d, N_MOD * d), b_mod).reshape(MOD_ROWS, 1, N_MOD * d)

    def bf(a):
        return a.reshape(a.shape[1:]).astype(BF16)

    w = bf(w_in)
    lr_w = 2 * GLA_RANK
    wgk = jnp.zeros((LR_PAD, 2 * BRANCH_W), F32)
    wgk = wgk.at[:GLA_RANK, :BRANCH_W].set(gla_w_gk[0, 0]).at[GLA_RANK:lr_w, BRANCH_W:].set(gla_w_gk[0, 1])
    bgk = gla_b_gk.reshape(1, 2 * BRANCH_W)

    tail_weights = (w_br_hg, w_br_gla, w_out, w_ff_gate, w_ff_up, w_ff_down)

    def in_projection(exact):
        return _in_projection(x, ctx, mod3, norm_pre1, w, hg_lb, wgk.astype(BF16), bgk,
                              () if exact else tail_weights, n_streams=IN_STREAMS, exact=exact)

    *fast, excursion, sbw0, wbh, wbg, wo, wfg, wfu, wfd, wgh, wgg, wgm = in_projection(False)
    tail_bf16 = (wgh, wgg, wgm, wbh, wbg, wo, wfg, wfu, wfd)

    def rest(qdb, kdb, v, elb, oi, sbw0):
        return _tail(oi, qdb, kdb, v, elb, sbw0, x, mod3,
                     hg_onorm, gla_onorm, norm_pre1, norm_post1, norm_pre2, norm_post2,
                     *tail_bf16, tm=TAIL_ROWS)

    def redo_exact():
        qdb, kdb, v, elb, oi, _, sbw0_exact = in_projection(True)[:7]
        return rest(qdb, kdb, v, elb, oi, sbw0_exact)

    return lax.cond(jnp.max(excursion) > EXCURSION_LIMIT_LOG2, redo_exact, lambda: rest(*fast, sbw0))
```

```python
import functools

import jax
import jax.numpy as jnp
from jax import lax
from jax.experimental import pallas as pl
from jax.experimental.pallas import tpu as pltpu

F32 = jnp.float32
BF16 = jnp.bfloat16

EPS = 1e-6
LOG2E = 1.4426950408889634
CHUNK = 64
N_MOD = 6
HEADS = 4
HEAD_DIM = 128
BRANCH_W = HEADS * HEAD_DIM
DIR_W = 2 * BRANCH_W
GLA_RANK = 16
GLA_GATE_NORM = 16.0
LR_PAD = 128
G_HQ, G_HI, G_HF, G_HB, G_HGATE, G_GQ, G_GK, G_GV, G_GGATE = range(9)
LR_COL = 9 * BRANCH_W
GATE_COLS = ((G_HGATE * BRANCH_W, (G_HGATE + 1) * BRANCH_W), (G_GGATE * BRANCH_W, (G_GGATE + 1) * BRANCH_W),
             (LR_COL + 2 * GLA_RANK, LR_COL + 2 * GLA_RANK + 4 * BRANCH_W))
MOD_ROWS = 16
VMEM_LIMIT_BYTES = 56 * 1024 * 1024
TAIL_ROWS = 512
TAIL_STREAMS = 2
IN_STREAMS = 2
CAST_ROW_TILE = 16
EXCURSION_LIMIT_LOG2 = 96.0


def _dot(a, b):
    return jnp.dot(a, b, preferred_element_type=F32)


def _dot_nt(a, b):
    return lax.dot_general(a, b, (((1,), (1,)), ((), ())), preferred_element_type=F32)


def _dot_tn(a, b):
    return lax.dot_general(a, b, (((0,), (0,)), ((), ())), preferred_element_type=F32)


def _sigmoid(x):
    return 1.0 / (1.0 + jnp.exp(-x))


def _silu(x):
    return x * _sigmoid(x)


def _rms(x):
    return x * lax.rsqrt(jnp.mean(x * x, axis=-1, keepdims=True) + EPS)


def _const_spec(shape):
    nd = len(shape)
    return pl.BlockSpec(shape, lambda *_: (0,) * nd, pipeline_mode=pl.Buffered(1))


def _mod_kernel(c_ref, w_ref, b_ref, o_ref):
    a = _silu(c_ref[...]).astype(BF16)
    o_ref[:, 0, :] = _dot(a, w_ref[...].astype(BF16)) + b_ref[...]


def _modulation(c_all, w_mod, b_mod):
    d = c_all.shape[1]
    n = w_mod.shape[1]
    blk = 1024
    return pl.pallas_call(
        _mod_kernel,
        grid=(n // blk,),
        in_specs=[pl.BlockSpec((MOD_ROWS, d), lambda j: (0, 0)),
                  pl.BlockSpec((d, blk), lambda j: (0, j)),
                  pl.BlockSpec((1, blk), lambda j: (0, j))],
        out_specs=pl.BlockSpec((MOD_ROWS, 1, blk), lambda j: (0, 0, j)),
        out_shape=jax.ShapeDtypeStruct((MOD_ROWS, 1, n), F32),
        name="modulation",
    )(c_all, w_mod, b_mod)


def _state_step(st, e_last, v, kd):
    return st * e_last + _dot_tn(v, kd)


def _inproj_kernel(x_ref, ctx_ref, mod_ref, pre_ref, w_ref, lb_ref, wgk_ref, bgk_ref, *rest,
                   d_model, rows_per_stream, n_streams, exact, n_cast, n_gate):
    cast_in, rest = rest[:n_cast], rest[n_cast:]
    qdb_ref, kdb_ref, v_ref, elb_ref, oi_ref, rng_ref, sbw0_ref = rest[:7]
    cast_out, rest = rest[7:7 + n_cast], rest[7 + n_cast:]
    gate_out, rest = rest[:n_gate], rest[n_gate:]
    sfw_ref, scratch = rest[0], rest[1:]
    rs = rows_per_stream
    n_ch = rs // CHUNK
    mids = (CHUNK // 2, CHUNK // 2 - 1)
    lasts = (CHUNK - 1, 0)

    def body(z_ref, latent):
        if latent:
            for wi_ref, wo_ref in zip(cast_in, cast_out):
                wo_ref[...] = wi_ref[...].astype(BF16)
            if n_gate:
                slab = gate_out[0].shape[0]
                step = pl.program_id(0) * (pl.num_programs(1) - 1) + pl.program_id(1) - 1
                r0 = pl.multiple_of(step * slab, slab)
                for (c0, c1), go_ref in zip(GATE_COLS, gate_out):
                    go_ref[...] = w_ref[pl.ds(r0, slab), c0:c1]
        else:
            sfw_ref[...] = jnp.zeros_like(sfw_ref)
        mod = mod_ref[0]
        shift, scale = mod[:, :d_model], mod[:, d_model:2 * d_model]

        lbr = lb_ref[...]
        e = jnp.exp(lbr - jnp.max(lbr, axis=0, keepdims=True))
        lb = e[0] / jnp.sum(e, axis=0)

        row = lax.broadcasted_iota(jnp.int32, (rs, rs), 0)
        col = lax.broadcasted_iota(jnp.int32, (rs, rs), 1)
        shift_bits = CHUNK.bit_length() - 1
        same = jnp.right_shift(row, shift_bits) == jnp.right_shift(col, shift_bits)
        masks = (same & (col <= row), same & (col >= row))
        tris = tuple(jnp.where(m, 1.0, 0.0).astype(BF16) for m in masks)

        def proj(t, g):
            return _dot(t['h'], w_ref[:, g * BRANCH_W:(g + 1) * BRANCH_W])

        span = []

        def prep(q, k, lf, d):
            lf2 = lf * LOG2E
            hi = lf2.astype(BF16)
            md = (lf2 - hi.astype(F32)).astype(BF16)
            c = _dot(tris[d], hi) + _dot(tris[d], md)
            qts, kts, qds, kds, els = [], [], [], [], []
            for n in range(n_ch):
                rows = slice(n * CHUNK, (n + 1) * CHUNK)
                cj = c[rows]
                cl = cj[lasts[d]:lasts[d] + 1]
                els.append(jnp.exp2(cl))
                if exact or not latent:
                    kds.append((k[rows] * jnp.exp2(cl - cj)).astype(BF16))
                    if latent:
                        qds.append((q[rows] * jnp.exp2(cj)).astype(BF16))
                    continue
                r = cj[mids[d]:mids[d] + 1]
                dq = cj - r
                span.append(jnp.maximum(jnp.abs(dq[0:1]), jnp.abs(dq[CHUNK - 1:CHUNK])))
                qt = q[rows] * jnp.exp2(dq)
                kt = k[rows] * jnp.exp2(-dq)
                qds.append((qt * jnp.exp2(r)).astype(BF16))
                kds.append((kt * jnp.exp2(cl - r)).astype(BF16))
                qts.append(qt.astype(BF16))
                kts.append(kt.astype(BF16))
            scores = None
            if latent:
                scores = (q, k, c) if exact else (jnp.concatenate(qts, axis=0), jnp.concatenate(kts, axis=0))
            return dict(qd=qds, kd=kds, el=els, scores=scores)

        def intra_exact(qkc, v, d):
            q, k, c = qkc
            k_s, v_s, c_s = scratch
            k_s[...] = k
            v_s[...] = v.astype(F32)
            c_s[...] = c
            rowid = lax.broadcasted_iota(jnp.int32, (CHUNK, 1), 0)
            outs = []
            for n in range(n_ch):
                base = n * CHUNK
                qj = q[base:base + CHUNK]
                cj = c[base:base + CHUNK]

                def step(s, acc):
                    cs = c_s[pl.ds(base + s, 1), :]
                    ks = k_s[pl.ds(base + s, 1), :]
                    vs = v_s[pl.ds(base + s, 1), :]
                    at_or_after = (rowid >= s) if d == 0 else (rowid <= s)
                    p = jnp.where(at_or_after, qj * jnp.exp2(jnp.minimum(cj - cs, 0.0)) * ks, 0.0)
                    parts = []
                    for hd in range(HEADS):
                        sl = slice(hd * HEAD_DIM, (hd + 1) * HEAD_DIM)
                        parts.append(jnp.sum(p[:, sl], axis=-1, keepdims=True) * vs[:, sl])
                    return acc + jnp.concatenate(parts, axis=-1)

                outs.append(lax.fori_loop(0, CHUNK, step, jnp.zeros((CHUNK, BRANCH_W), F32)))
            return jnp.concatenate(outs, axis=0)

        def intra(p, v, d):
            if not latent:
                return None
            if exact:
                return intra_exact(p['scores'], v, d)
            qt, kt = p['scores']
            outs = []
            for hd in range(HEADS):
                sl = slice(hd * HEAD_DIM, (hd + 1) * HEAD_DIM)
                a = _dot_nt(qt[:, sl], kt[:, sl])
                a = jnp.where(masks[d], a, 0.0).astype(BF16)
                outs.append(_dot(a, v[:, sl]))
            return jnp.concatenate(outs, axis=-1)

        def scan_fw(p, v, branch):
            upd = {}
            for n in range(n_ch):
                rows = slice(n * CHUNK, (n + 1) * CHUNK)
                for hd in range(HEADS):
                    sl = slice(hd * HEAD_DIM, (hd + 1) * HEAD_DIM)
                    upd[n, hd] = _dot_tn(v[rows, sl], p['kd'][n][:, sl])
            st = [sfw_ref[branch * HEADS + hd] for hd in range(HEADS)]
            outs = []
            for n in range(n_ch):
                parts = []
                for hd in range(HEADS):
                    sl = slice(hd * HEAD_DIM, (hd + 1) * HEAD_DIM)
                    if latent:
                        parts.append(_dot_nt(p['qd'][n][:, sl], st[hd].astype(BF16)))
                    st[hd] = st[hd] * p['el'][n][:, sl] + upd[n, hd]
                if latent:
                    outs.append(jnp.concatenate(parts, axis=-1))
            for hd in range(HEADS):
                sfw_ref[branch * HEADS + hd] = st[hd]
            return jnp.concatenate(outs, axis=0) if latent else None

        def emit_bw(t, p, v, branch):
            c0 = branch * BRANCH_W
            if latent:
                for n in range(n_ch):
                    rows = slice(t['first'] + n * CHUNK, t['first'] + (n + 1) * CHUNK)
                    qdb_ref[0, rows, c0:c0 + BRANCH_W] = p['qd'][n]
                    kdb_ref[0, rows, c0:c0 + BRANCH_W] = p['kd'][n]
                    elb_ref[0, t['first'] // CHUNK + n, :, c0:c0 + BRANCH_W] = p['el'][n]
                return
            for hd in range(HEADS):
                sl = slice(hd * HEAD_DIM, (hd + 1) * HEAD_DIM)
                st = jnp.zeros((HEAD_DIM, HEAD_DIM), F32)
                for n in reversed(range(n_ch)):
                    rows = slice(n * CHUNK, (n + 1) * CHUNK)
                    st = _state_step(st, p['el'][n][:, sl], v[rows, sl], p['kd'][n][:, sl])
                sbw0_ref[0, branch * HEADS + hd] = st

        def hg_gate(raw, d):
            lbd = lb[d:d + 1]
            f = lbd + (1.0 - lbd) * _sigmoid(raw)
            return 1.0 - f, jnp.log(f)

        def total(*parts):
            return functools.reduce(lambda a, b: a + b, parts).astype(BF16)

        def s_input(t):
            t['h'] = (_rms(z_ref[0, t['rows'], :]) * pre_ref[...] * (1.0 + scale) + shift).astype(BF16)

        def s_hg_proj(t):
            pq_h = proj(t, G_HQ) if latent else None
            pv_h = proj(t, G_HI)
            t['q_h'] = _silu(pq_h) if latent else None
            t['v_h'] = pv_h.astype(BF16)

        def s_hg_proj_f(t):
            t['pf'] = [proj(t, G_HF + d) for d in range(2)]

        def s_hg_fw(t):
            t['p_h0'] = prep(t['q_h'], *hg_gate(t['pf'][0], 0), 0)

        def s_gla_proj_qk(t):
            t['pq_g'] = proj(t, G_GQ) if latent else None
            t['pk_g'] = proj(t, G_GK)

        def s_gla_proj(t):
            t['pv_g'] = proj(t, G_GV)
            lr = _dot(t['h'], w_ref[:, LR_COL:LR_COL + LR_PAD]).astype(BF16)
            t['xg'] = _dot(lr, wgk_ref[...]) + bgk_ref[...]

        def s_hg_bw(t):
            t['p_h1'] = prep(t['q_h'], *hg_gate(t['pf'][1], 1), 1)
            emit_bw(t, t['p_h1'], t['v_h'], 0)

        def s_hg_intra_fw(t):
            t['o0'] = intra(t['p_h0'], t['v_h'], 0)

        def s_gla_fw(t):
            t['q_g'] = t['pq_g'] * HEAD_DIM ** -0.5 if latent else None
            t['v_g'] = t['pv_g'].astype(BF16)
            if latent:
                v_ref[0, t['rows'], 0:BRANCH_W] = t['v_h']
                v_ref[0, t['rows'], BRANCH_W:2 * BRANCH_W] = t['v_g']
            xg = t['xg']
            t['ls'] = (jnp.minimum(xg, 0.0) - jnp.log(1.0 + jnp.exp(-jnp.abs(xg)))) * (1.0 / GLA_GATE_NORM)
            t['p_g0'] = prep(t['q_g'], t['pk_g'], t['ls'][:, :BRANCH_W], 0)

        def s_hg_scan(t):
            t['os_h'] = scan_fw(t['p_h0'], t['v_h'], 0)

        def s_hg_finish(t):
            o1 = intra(t['p_h1'], t['v_h'], 1)
            if latent:
                oi_ref[0, t['rows'], 0:BRANCH_W] = total(t['o0'], o1, t['os_h'])

        def s_gla_bw(t):
            t['p_g1'] = prep(t['q_g'], t['pk_g'], t['ls'][:, BRANCH_W:], 1)
            emit_bw(t, t['p_g1'], t['v_g'], 1)

        def s_gla_intra_fw(t):
            t['og0'] = intra(t['p_g0'], t['v_g'], 0)

        def s_gla_intra_bw(t):
            t['og1'] = intra(t['p_g1'], t['v_g'], 1)

        def s_gla_finish(t):
            os0 = scan_fw(t['p_g0'], t['v_g'], 1)
            if latent:
                oi_ref[0, t['rows'], BRANCH_W:2 * BRANCH_W] = total(t['og0'], t['og1'], os0)

        n_here = n_streams if latent else 1
        streams = [dict(first=i * rs, rows=slice(i * rs, (i + 1) * rs)) for i in range(n_here)]
        for stage in (s_input, s_hg_proj, s_hg_proj_f, s_hg_fw, s_gla_proj_qk, s_gla_proj, s_hg_bw,
                      s_hg_intra_fw, s_gla_fw, s_hg_scan, s_hg_finish, s_gla_bw, s_gla_intra_fw,
                      s_gla_intra_bw, s_gla_finish):
            for t in streams:
                stage(t)

        if latent:
            widest = jnp.zeros((1, BRANCH_W), F32)
            for piece in span:
                widest = jnp.maximum(widest, piece)
            lanes = widest[:, 0:HEAD_DIM]
            for hd in range(1, HEADS):
                lanes = jnp.maximum(lanes, widest[:, hd * HEAD_DIM:(hd + 1) * HEAD_DIM])
            rng_ref[0, 0] = jnp.maximum(rng_ref[0, 0], lanes)
        else:
            @pl.when(pl.program_id(0) == 0)
            def _():
                rng_ref[...] = jnp.zeros_like(rng_ref)

    j = pl.program_id(1)

    @pl.when(j == 0)
    def _():
        body(ctx_ref, False)

    @pl.when(j > 0)
    def _():
        body(x_ref, True)


def _in_projection(x, ctx, mod3, norm_pre, w_in_r, hg_lb, wgk, bgk, cast_weights, *, n_streams, exact):
    b, seq, d = x.shape
    rs = ctx.shape[1]
    tm = n_streams * rs
    assert rs % CHUNK == 0 and seq % tm == 0
    n_lat = seq // tm
    nw = w_in_r.shape[1]
    n_cast = len(cast_weights)
    cast_2d = [cw.reshape(cw.shape[-2:]) for cw in cast_weights]
    n_gate = 0 if exact else len(GATE_COLS)
    steps = b * n_lat
    assert d % (steps * CAST_ROW_TILE) == 0
    gate_specs = [pl.BlockSpec((d // steps, c1 - c0), lambda i, j: (i * n_lat + jnp.maximum(j - 1, 0), 0))
                  for c0, c1 in GATE_COLS[:n_gate]]
    gate_shapes = [jax.ShapeDtypeStruct((d, c1 - c0), BF16) for c0, c1 in GATE_COLS[:n_gate]]
    kern = functools.partial(_inproj_kernel, d_model=d, rows_per_stream=rs, n_streams=n_streams,
                             exact=exact, n_cast=n_cast, n_gate=n_gate)

    def cast_spec(cw):
        tiles = cw.shape[0] // CAST_ROW_TILE
        n_slabs = max(n for n in range(1, b * n_lat + 1) if tiles % n == 0)
        return pl.BlockSpec(
            (cw.shape[0] // n_slabs, cw.shape[1]),
            lambda i, j: (jnp.minimum(i * n_lat + jnp.maximum(j - 1, 0), n_slabs - 1), 0))

    def latent_rows(w):
        return pl.BlockSpec((1, tm, w), lambda i, j: (i, jnp.maximum(j - 1, 0), 0))

    n_state = 2 * HEADS
    return pl.pallas_call(
        kern,
        grid=(b, n_lat + 1),
        in_specs=[
            latent_rows(d),
            pl.BlockSpec((1, rs, d), lambda i, j: (i, 0, 0)),
            pl.BlockSpec((1, 1, N_MOD * d), lambda i, j: (jnp.where(j == 0, 0, i + 1), 0, 0)),
            _const_spec((1, d)),
            _const_spec((d, nw)),
            _const_spec(hg_lb.shape),
            _const_spec(wgk.shape),
            _const_spec(bgk.shape),
        ] + [cast_spec(cw) for cw in cast_2d],
        out_specs=[
            latent_rows(DIR_W),
            latent_rows(DIR_W),
            latent_rows(DIR_W),
            pl.BlockSpec((1, tm // CHUNK, 1, DIR_W), lambda i, j: (i, jnp.maximum(j - 1, 0), 0, 0)),
            latent_rows(DIR_W),
            pl.BlockSpec((1, 1, 1, HEAD_DIM), lambda i, j: (0, 0, 0, 0)),
            pl.BlockSpec((1, n_state, HEAD_DIM, HEAD_DIM), lambda i, j: (i, 0, 0, 0)),
        ] + [cast_spec(cw) for cw in cast_2d] + gate_specs,
        out_shape=[
            jax.ShapeDtypeStruct((b, seq, DIR_W), BF16),
            jax.ShapeDtypeStruct((b, seq, DIR_W), BF16),
            jax.ShapeDtypeStruct((b, seq, DIR_W), BF16),
            jax.ShapeDtypeStruct((b, seq // CHUNK, 1, DIR_W), F32),
            jax.ShapeDtypeStruct((b, seq, DIR_W), BF16),
            jax.ShapeDtypeStruct((1, 1, 1, HEAD_DIM), F32),
            jax.ShapeDtypeStruct((b, n_state, HEAD_DIM, HEAD_DIM), F32),
        ] + [jax.ShapeDtypeStruct(cw.shape, BF16) for cw in cast_2d] + gate_shapes,
        scratch_shapes=([pltpu.VMEM((n_state, HEAD_DIM, HEAD_DIM), F32)]
                        + ([pltpu.VMEM((rs, BRANCH_W), F32)] * 3 if exact else [])),
        compiler_params=pltpu.CompilerParams(
            dimension_semantics=("arbitrary", "arbitrary"),
            vmem_limit_bytes=VMEM_LIMIT_BYTES),
        name="in_projection_exact" if exact else "in_projection",
    )(x, ctx, mod3, norm_pre, w_in_r, hg_lb, wgk, bgk, *cast_2d)


def _tail_kernel(oi_ref, qdb_ref, kdb_ref, v_ref, elb_ref, sbw0_ref, x_ref, mod_ref,
                 hgn_ref, gln_ref, pre1_ref, post1_ref, pre2_ref, post2_ref,
                 wgh_ref, wgg_ref, wgm_ref, wbh_ref, wbg_ref, wout_ref, wg_ref, wu_ref, wd_ref,
                 out_ref, sbw_ref, *, d_model, n_streams):
    mod = mod_ref[0]
    m = [mod[:, i * d_model:(i + 1) * d_model] for i in range(N_MOD)]
    rows_per_stream = x_ref.shape[1] // n_streams
    chunks_per_stream = rows_per_stream // CHUNK
    gates = (wgh_ref, wgg_ref)
    norms = (hgn_ref, gln_ref)
    branch_w = (wbh_ref, wbg_ref)
    streams = [dict(first=i * rows_per_stream, rows=slice(i * rows_per_stream, (i + 1) * rows_per_stream))
               for i in range(n_streams)]

    @pl.when(pl.program_id(1) == 0)
    def _():
        sbw_ref[...] = sbw0_ref[0]

    def scan_bw(t):
        parts = [None] * chunks_per_stream
        for n in reversed(range(chunks_per_stream)):
            rows = slice(t['first'] + n * CHUNK, t['first'] + (n + 1) * CHUNK)
            e_last = elb_ref[0, t['first'] // CHUNK + n]
            heads = []
            for u in range(2 * HEADS):
                sl = slice(u * HEAD_DIM, (u + 1) * HEAD_DIM)
                st = sbw_ref[u]
                heads.append(_dot_nt(qdb_ref[0, rows, sl], st.astype(BF16)))
                sbw_ref[u] = _state_step(st, e_last[:, sl], v_ref[0, rows, sl], kdb_ref[0, rows, sl])
            parts[n] = jnp.concatenate(heads, axis=-1)
        t['o_bw'] = jnp.concatenate(parts, axis=0)

    def mixer_input(t):
        t['x'] = x_ref[0, t['rows'], :]
        t['h1'] = (_rms(t['x']) * pre1_ref[...] * (1.0 + m[1]) + m[0]).astype(BF16)

    def gate_proj(t):
        t['out_gate'] = [_silu(_dot(t['h1'], gates[br][...])) for br in range(2)]
        t['merge_gate'] = _sigmoid(_dot(t['h1'], wgm_ref[...]))

    def branch_proj(t):
        o = oi_ref[0, t['rows'], :].astype(F32) + t['o_bw']
        t['ys'] = []
        for br in range(2):
            parts = []
            for hd in range(HEADS):
                c0 = br * BRANCH_W + hd * HEAD_DIM
                parts.append(_rms(o[:, c0:c0 + HEAD_DIM]) * norms[br][...])
            og = (jnp.concatenate(parts, axis=-1) * t['out_gate'][br]).astype(BF16)
            t['ys'].append(_dot(og, branch_w[br][...]))

    def out_proj(t):
        gm = t['merge_gate']
        merged = gm[:, :d_model] * t['ys'][0] + gm[:, d_model:] * t['ys'][1]
        t['y'] = _dot(merged.astype(BF16), wout_ref[...])

    def ffn_input(t):
        t['z1'] = t['x'] + _rms(t['y']) * post1_ref[...] * m[2]
        t['h2'] = (_rms(t['z1']) * pre2_ref[...] * (1.0 + m[4]) + m[3]).astype(BF16)

    def ffn_up(t):
        g = _dot(t['h2'], wg_ref[...])
        u = _dot(t['h2'], wu_ref[...])
        t['a'] = (_silu(g) * u).astype(BF16)

    def ffn_down(t):
        t['y2'] = _dot(t['a'], wd_ref[...])

    def residual_out(t):
        out_ref[0, t['rows'], :] = t['z1'] + _rms(t['y2']) * post2_ref[...] * m[5]

    for t in reversed(streams):
        scan_bw(t)
    for stage in (mixer_input, gate_proj, branch_proj, out_proj, ffn_input, ffn_up, ffn_down, residual_out):
        for t in streams:
            stage(t)


def _tail(oi, qdb, kdb, v, elb, sbw0, x, mod3, *consts, tm):
    b, seq, d = x.shape
    n_tiles = seq // tm
    kern = functools.partial(_tail_kernel, d_model=d, n_streams=TAIL_STREAMS)

    def rows(w):
        return pl.BlockSpec((1, tm, w), lambda i, j: (i, n_tiles - 1 - j, 0))

    return pl.pallas_call(
        kern,
        grid=(b, n_tiles),
        in_specs=[
            rows(DIR_W), rows(DIR_W), rows(DIR_W), rows(DIR_W),
            pl.BlockSpec((1, tm // CHUNK, 1, DIR_W), lambda i, j: (i, n_tiles - 1 - j, 0, 0)),
            pl.BlockSpec((1,) + sbw0.shape[1:], lambda i, j: (i, 0, 0, 0)),
            rows(d),
            pl.BlockSpec((1, 1, N_MOD * d), lambda i, j: (i + 1, 0, 0)),
        ] + [_const_spec(c.shape) for c in consts],
        out_specs=rows(d),
        out_shape=jax.ShapeDtypeStruct((b, seq, d), F32),
        scratch_shapes=[pltpu.VMEM(sbw0.shape[1:], F32)],
        compiler_params=pltpu.CompilerParams(
            dimension_semantics=("arbitrary", "arbitrary"),
            vmem_limit_bytes=VMEM_LIMIT_BYTES),
        name="mixer_out_ffn",
    )(oi, qdb, kdb, v, elb, sbw0, x, mod3, *consts)


def kernel(x, c, ctx, c_ctx, w_mod, b_mod, norm_pre1, norm_post1, norm_pre2, norm_post2, w_in, hg_lb,
           hg_onorm, gla_w_gk, gla_b_gk, gla_onorm, w_br_hg, w_br_gla, w_out, w_ff_gate, w_ff_up, w_ff_down):
    b, seq, d = x.shape
    ctx_len = ctx.shape[1]
    assert w_mod.shape[0] == 1 and d == DIR_W
    assert seq % CHUNK == 0 and ctx_len % CHUNK == 0

    c_all = jnp.concatenate([c_ctx[None], c, jnp.zeros((MOD_ROWS - 1 - b, d), F32)], axis=0)
    mod3 = _modulation(c_all, w_mod.reshape(d, N_MOD * d), b_mod)

    def bf(a):
        return a.reshape(a.shape[1:]).astype(BF16)

    w = bf(w_in)
    lr_w = 2 * GLA_RANK
    zr = jnp.zeros((GLA_RANK, BRANCH_W), F32)
    wgk = jnp.concatenate([jnp.concatenate([gla_w_gk[0, 0], zr], axis=1),
                           jnp.concatenate([zr, gla_w_gk[0, 1]], axis=1),
                           jnp.zeros((LR_PAD - lr_w, 2 * BRANCH_W), F32)], axis=0)
    bgk = gla_b_gk.reshape(1, 2 * BRANCH_W)

    tail_weights = (w_br_hg, w_br_gla, w_out, w_ff_gate, w_ff_up, w_ff_down)

    def in_projection(exact):
        return _in_projection(x, ctx, mod3, norm_pre1, w, hg_lb, wgk.astype(BF16), bgk,
                              () if exact else tail_weights, n_streams=IN_STREAMS, exact=exact)

    *fast, excursion, sbw0, wbh, wbg, wo, wfg, wfu, wfd, wgh, wgg, wgm = in_projection(False)
    tail_bf16 = (wgh, wgg, wgm, wbh, wbg, wo, wfg, wfu, wfd)

    def rest(qdb, kdb, v, elb, oi, sbw0):
        return _tail(oi, qdb, kdb, v, elb, sbw0, x, mod3,
                     hg_onorm, gla_onorm, norm_pre1, norm_post1, norm_pre2, norm_post2,
                     *tail_bf16, tm=TAIL_ROWS)

    def redo_exact():
        qdb, kdb, v, elb, oi, _, sbw0_exact = in_projection(True)[:7]
        return rest(qdb, kdb, v, elb, oi, sbw0_exact)

    return lax.cond(jnp.max(excursion) > EXCURSION_LIMIT_LOG2, redo_exact, lambda: rest(*fast, sbw0))
```
